```python
import math
import jax, jax.numpy as jnp
from jax import lax
import numpy as np

D_MODEL = 1024
BATCH = 16
SEQ = 4096
DEPTH = 4

GLA_HEADS = 4
GLA_DV = D_MODEL // 8
GLA_DK = GLA_DV // 2
GLA_LOWRANK = 16
GLA_TAU = 16.0
GLA_CHUNK = 64
DIFF_HEADS = 4
DIFF_DV = D_MODEL // 8
DIFF_DQK = DIFF_DV // 2
Q_BLOCK = 128
REL_BUCKETS = 32
REL_MAX_DIST = 128
MIX_WIDTH = GLA_HEADS * GLA_DV + DIFF_HEADS * DIFF_DV
D_FF = -(-8 * D_MODEL // (3 * 256)) * 256
RMS_EPS = 1e-6

IN_SIZES = [
    GLA_HEADS * GLA_DK,
    GLA_HEADS * GLA_DK,
    GLA_HEADS * GLA_DV,
    GLA_HEADS * GLA_DV,
    GLA_LOWRANK,
    DIFF_HEADS * 2 * DIFF_DQK,
    DIFF_HEADS * 2 * DIFF_DQK,
    DIFF_HEADS * DIFF_DV,
]
IN_WIDTH = sum(IN_SIZES)
IN_SPLITS = [int(s) for s in np.cumsum(IN_SIZES)[:-1]]

kernel_name = "hymba_gla_diffattn_t5bias_sandwich_adaln"


def rms_norm(x, g):
    xf = x.astype(jnp.float32)
    y = xf * lax.rsqrt(jnp.mean(xf * xf, axis=-1, keepdims=True) + RMS_EPS)
    return (y * g.astype(jnp.float32)).astype(x.dtype)


def t5_bucket(rel):
    max_exact = REL_BUCKETS // 2
    n = jnp.maximum(rel, 0)
    nf = jnp.maximum(n, 1).astype(jnp.float32)
    large = max_exact + (jnp.log(nf / max_exact) / math.log(REL_MAX_DIST / max_exact)
                         * (REL_BUCKETS - max_exact)).astype(jnp.int32)
    large = jnp.minimum(large, REL_BUCKETS - 1)
    return jnp.where(n < max_exact, n, large)


def gla_chunked(q, k, v, log_a):
    B, S, H, dk = q.shape
    dv = v.shape[-1]
    C = GLA_CHUNK
    N = S // C

    def blk(t):
        return t.astype(jnp.float32).reshape(B, N, C, H, t.shape[-1]).transpose(0, 3, 1, 2, 4)

    q, k, v, log_a = blk(q) * (dk ** -0.5), blk(k), blk(v), blk(log_a)
    b = jnp.cumsum(log_a, axis=3)
    b_last = b[:, :, :, -1:, :]
    q_e = q * jnp.exp(b)
    k_e = k * jnp.exp(-b)
    causal = jnp.tril(jnp.ones((C, C), dtype=bool))
    a_intra = jnp.where(causal, jnp.einsum('bhncd,bhnjd->bhncj', q_e, k_e), 0.0)
    o_intra = jnp.einsum('bhncj,bhnjv->bhncv', a_intra, v)
    u = jnp.einsum('bhncd,bhncv->bhndv', k * jnp.exp(b_last - b), v)
    decay = jnp.exp(b_last[:, :, :, 0, :])

    def step(state, inp):
        u_n, d_n = inp
        return d_n[..., None] * state + u_n, state

    _, s_prev = lax.scan(step, jnp.zeros((B, H, dk, dv), jnp.float32),
                         (jnp.moveaxis(u, 2, 0), jnp.moveaxis(decay, 2, 0)))
    s_prev = jnp.moveaxis(s_prev, 0, 2)
    o = o_intra + jnp.einsum('bhncd,bhndv->bhncv', q_e, s_prev)
    return o.transpose(0, 2, 3, 1, 4).reshape(B, S, H, dv)


def diff_attention(q, k, v, rel_bias, lam):
    S = q.shape[1]
    scale = DIFF_DQK ** -0.5
    bias_tab = rel_bias.astype(jnp.float32)
    outs = []
    for i in range(S // Q_BLOCK):
        q0 = i * Q_BLOCK
        end = q0 + Q_BLOCK
        s = jnp.einsum('bqhmd,bkhmd->bhmqk', q[:, q0:end], k[:, :end]).astype(jnp.float32) * scale
        rel = jnp.arange(q0, end)[:, None] - jnp.arange(end)[None, :]
        bias = jnp.transpose(bias_tab[t5_bucket(rel)], (2, 0, 1))
        s = s + bias[None, :, None]
        s = jnp.where(rel >= 0, s, -jnp.inf)
        p = jax.nn.softmax(s, axis=-1)
        a = p[:, :, 0] - lam * p[:, :, 1]
        outs.append(jnp.einsum('bhqk,bkhv->bqhv', a.astype(v.dtype), v[:, :end]))
    return jnp.concatenate(outs, axis=1)


def token_mixer(h, w_in, w_gla_gate, b_gla_gate, gla_norm, diff_lambda, diff_norm,
                rel_bias, w_out, lam_init):
    B, S, _ = h.shape
    proj = h @ w_in
    gq, gk, gv, gr, ga, dq, dk, dv = jnp.split(proj, IN_SPLITS, axis=-1)
    log_a = jax.nn.log_sigmoid((ga @ w_gla_gate + b_gla_gate).astype(jnp.float32)) / GLA_TAU
    o_g = gla_chunked(gq.reshape(B, S, GLA_HEADS, GLA_DK), gk.reshape(B, S, GLA_HEADS, GLA_DK),
                      gv.reshape(B, S, GLA_HEADS, GLA_DV), log_a.reshape(B, S, GLA_HEADS, GLA_DK))
    o_g = rms_norm(o_g, gla_norm) * jax.nn.silu(gr.reshape(B, S, GLA_HEADS, GLA_DV).astype(jnp.float32))
    lf = diff_lambda.astype(jnp.float32)
    lam = jnp.exp(jnp.sum(lf[0] * lf[1])) - jnp.exp(jnp.sum(lf[2] * lf[3])) + lam_init
    o_d = diff_attention(dq.reshape(B, S, DIFF_HEADS, 2, DIFF_DQK), dk.reshape(B, S, DIFF_HEADS, 2, DIFF_DQK),
                         dv.reshape(B, S, DIFF_HEADS, DIFF_DV), rel_bias, lam)
    o_d = rms_norm(o_d.astype(jnp.float32), diff_norm) * (1.0 - lam_init)
    o = jnp.concatenate([o_g.reshape(B, S, -1), o_d.reshape(B, S, -1)], axis=-1).astype(h.dtype)
    return o @ w_out


def swiglu(h, w_gate_up, w_down):
    gate, up = jnp.split(h @ w_gate_up, 2, axis=-1)
    return (jax.nn.silu(gate) * up) @ w_down


def setup_inputs(seed: int = 0) -> dict:
    key = jax.random.key(seed)
    ks = jax.random.split(key, 16)
    f32 = jnp.float32
    nrm = lambda k, shape, s: jax.random.normal(k, shape, f32) * s
    return {
        "x": nrm(ks[0], (BATCH, SEQ, D_MODEL), 1.0),
        "c": nrm(ks[1], (BATCH, D_MODEL), 1.0),
        "w_mod": nrm(ks[2], (DEPTH, D_MODEL, 6 * D_MODEL), 0.5 * D_MODEL ** -0.5),
        "b_mod": nrm(ks[3], (DEPTH, 6 * D_MODEL), 0.02),
        "norm_gains": 1.0 + nrm(ks[4], (DEPTH, 4, D_MODEL), 0.05),
        "w_in": nrm(ks[5], (DEPTH, D_MODEL, IN_WIDTH), D_MODEL ** -0.5),
        "w_gla_gate": nrm(ks[6], (DEPTH, GLA_LOWRANK, GLA_HEADS * GLA_DK), GLA_LOWRANK ** -0.5),
        "b_gla_gate": nrm(ks[7], (DEPTH, GLA_HEADS * GLA_DK), 0.1),
        "gla_norm": 1.0 + nrm(ks[8], (DEPTH, GLA_DV), 0.05),
        "diff_lambda": nrm(ks[9], (DEPTH, 4, DIFF_DQK), 0.1),
        "diff_norm": 1.0 + nrm(ks[10], (DEPTH, DIFF_DV), 0.05),
        "rel_bias": nrm(ks[11], (REL_BUCKETS, DIFF_HEADS), 0.5),
        "w_out": nrm(ks[12], (DEPTH, MIX_WIDTH, D_MODEL), MIX_WIDTH ** -0.5),
        "w_gate_up": nrm(ks[13], (DEPTH, D_MODEL, 2 * D_FF), D_MODEL ** -0.5),
        "w_down": nrm(ks[14], (DEPTH, D_FF, D_MODEL), D_FF ** -0.5),
    }


def reference(x, c, w_mod, b_mod, norm_gains, w_in, w_gla_gate, b_gla_gate, gla_norm,
              diff_lambda, diff_norm, rel_bias, w_out, w_gate_up, w_down):
    cond = jax.nn.silu(c)
    for l in range(DEPTH):
        lam_init = 0.8 - 0.6 * math.exp(-0.3 * l)
        mod = (cond @ w_mod[l] + b_mod[l])[:, None, :]
        sh1, sc1, g1, sh2, sc2, g2 = jnp.split(mod, 6, axis=-1)
        h = rms_norm(x, norm_gains[l, 0]) * (1.0 + sc1) + sh1
        y = token_mixer(h, w_in[l], w_gla_gate[l], b_gla_gate[l], gla_norm[l], diff_lambda[l],
                        diff_norm[l], rel_bias, w_out[l], lam_init)
        x = x + g1 * rms_norm(y, norm_gains[l, 1])
        h = rms_norm(x, norm_gains[l, 2]) * (1.0 + sc2) + sh2
        y = swiglu(h, w_gate_up[l], w_down[l])
        x = x + g2 * rms_norm(y, norm_gains[l, 3])
    return x
```

```python
import functools
import math

import numpy as np
import jax
import jax.numpy as jnp
from jax import lax
from jax.experimental import pallas as pl
from jax.experimental.pallas import tpu as pltpu

F32 = jnp.float32
BF16 = jnp.bfloat16

D_MODEL = 1024
DEPTH = 4
GLA_HEADS = 4
GLA_DV = 128
GLA_DK = 64
GLA_LOWRANK = 16
GLA_TAU = 16.0
GLA_CHUNK = 64
DIFF_HEADS = 4
DIFF_DV = 128
DIFF_DQK = 64
REL_BUCKETS = 32
REL_MAX_DIST = 128
D_FF = 2816
RMS_EPS = 1e-6
LOG2E = 1.4426950408889634

V7X_VMEM_BYTES = 64 * 1024 * 1024
VMEM_LIMIT_BYTES = V7X_VMEM_BYTES - 8 * 1024 * 1024
LANES = 128

OFF_GQ, OFF_GK, OFF_GV, OFF_GR = 0, 256, 512, 1024
OFF_DQ, OFF_DK, OFF_DV, OFF_GA = 1536, 2048, 2560, 3072
PROJ_W = 3200
PROJ_CHUNKS = ((0, 512), (512, 512), (1024, 512), (1536, 512), (2048, 512), (2560, 512), (3072, 128))

TM_PROJ = 512
TM_POST = 512
T_GLA = 512
TQ = 256
TK = 256
FF_CHUNKS = ((0, 512), (512, 512), (1024, 512), (1536, 512), (2048, 512), (2560, 256))
MASK_VALUE = -1e30

_NT = (((1,), (1,)), ((), ()))
_TN = (((0,), (0,)), ((), ()))


def _params(*sem):
    return pltpu.CompilerParams(dimension_semantics=sem, vmem_limit_bytes=VMEM_LIMIT_BYTES)


def _rms(x, g):
    ms = jnp.mean(x * x, axis=-1, keepdims=True)
    return x * lax.rsqrt(ms + RMS_EPS) * g


def _silu(x):
    return x * jax.nn.sigmoid(x)


def _mod_kernel(c_ref, w_ref, b_ref, o_ref):
    cond = _silu(c_ref[...]).astype(BF16)
    o_ref[0] = jnp.dot(cond, w_ref[0].astype(BF16), preferred_element_type=F32) + b_ref[0]


def _modulation(c, w_mod, b_mod):
    B = c.shape[0]
    n = w_mod.shape[-1]
    tn = n // 4
    return pl.pallas_call(
        _mod_kernel,
        out_shape=jax.ShapeDtypeStruct((DEPTH, B, n), F32),
        grid=(DEPTH, n // tn),
        in_specs=[
            pl.BlockSpec((B, D_MODEL), lambda l, j: (0, 0)),
            pl.BlockSpec((1, D_MODEL, tn), lambda l, j: (l, 0, j)),
            pl.BlockSpec((1, 1, tn), lambda l, j: (l, 0, j)),
        ],
        out_specs=pl.BlockSpec((1, B, tn), lambda l, j: (l, 0, j)),
        compiler_params=_params("parallel", "parallel"),
        name="adaln_modulation",
    )(c, w_mod, b_mod.reshape(DEPTH, 1, n))


def _inproj_kernel(x_ref, sc_ref, sh_ref, g_ref, w_ref, o_ref):
    h = _rms(x_ref[0], g_ref[...]) * (1.0 + sc_ref[0]) + sh_ref[0]
    hb = h.astype(BF16)
    for off, width in PROJ_CHUNKS:
        o_ref[0, :, off:off + width] = jnp.dot(
            hb, w_ref[:, off:off + width], preferred_element_type=F32).astype(BF16)


def _inproj(x, sc, sh, gain, w):
    B, S, D = x.shape
    vec = pl.BlockSpec((1, 1, D), lambda b, i: (b, 0, 0))
    return pl.pallas_call(
        _inproj_kernel,
        out_shape=jax.ShapeDtypeStruct((B, S, PROJ_W), BF16),
        grid=(B, S // TM_PROJ),
        in_specs=[
            pl.BlockSpec((1, TM_PROJ, D), lambda b, i: (b, i, 0)),
            vec, vec,
            pl.BlockSpec((1, D), lambda b, i: (0, 0)),
            pl.BlockSpec((D, PROJ_W), lambda b, i: (0, 0), pipeline_mode=pl.Buffered(1)),
        ],
        out_specs=pl.BlockSpec((1, TM_PROJ, PROJ_W), lambda b, i: (b, i, 0)),
        compiler_params=_params("parallel", "parallel"),
        name="in_projection",
    )(x, sc, sh, gain, w)


def _gla_kernel(q_ref, k_ref, v_ref, r_ref, a_ref, wg_ref, bg_ref, gn_ref, o_ref,
                st_ref, qe_ref, ke_ref, kd_ref, dec_ref):
    C = GLA_CHUNK
    n_chunks = T_GLA // C

    @pl.when(pl.program_id(1) == 0)
    def _():
        st_ref[...] = jnp.zeros_like(st_ref)

    z = jnp.dot(a_ref[0], wg_ref[...], preferred_element_type=F32) + bg_ref[...]
    la = (jnp.minimum(z, 0.0) - jnp.log1p(jnp.exp(-jnp.abs(z)))) * (1.0 / GLA_TAU)
    la_hi = la.astype(BF16)
    la_lo = (la - la_hi.astype(F32)).astype(BF16)
    row = lax.broadcasted_iota(jnp.int32, (C, C), 0)
    col = lax.broadcasted_iota(jnp.int32, (C, C), 1)
    tril = row >= col
    ltri = jnp.where(tril, 1.0, 0.0).astype(BF16)
    q = q_ref[0].astype(F32)
    k = k_ref[0].astype(F32)
    for c in range(n_chunks):
        sl = slice(c * C, (c + 1) * C)
        b = (jnp.dot(ltri, la_hi[sl], preferred_element_type=F32)
             + jnp.dot(ltri, la_lo[sl], preferred_element_type=F32))
        b_last = b[C - 1:C, :]
        qe_ref[sl, :] = (q[sl] * (GLA_DK ** -0.5) * jnp.exp(b)).astype(BF16)
        ke_ref[sl, :] = (k[sl] * jnp.exp(-b)).astype(BF16)
        kd_ref[sl, :] = (k[sl] * jnp.exp(b_last - b)).astype(BF16)
        dec_ref[c] = jnp.exp(b_last)

    lane = lax.broadcasted_iota(jnp.int32, (C, LANES), 1)
    first_head = lane < GLA_DK
    st_first = lax.broadcasted_iota(jnp.int32, (GLA_DV, LANES), 1) < GLA_DK

    def chunk_step(c, carry):
        r0 = pl.multiple_of(c * C, C)
        rows = pl.ds(r0, C)
        dec = dec_ref[c]
        for p in range(GLA_HEADS // 2):
            cols = slice(p * LANES, (p + 1) * LANES)
            qe = qe_ref[rows, cols]
            ke = ke_ref[rows, cols]
            kd = kd_ref[rows, cols]
            st = st_ref[p]
            st_b = st.astype(BF16)
            ut = []
            for j in range(2):
                hcols = slice((2 * p + j) * GLA_DV, (2 * p + j + 1) * GLA_DV)
                qm = jnp.where(first_head if j == 0 else ~first_head, qe, jnp.zeros_like(qe))
                a = lax.dot_general(qm, ke, _NT, preferred_element_type=F32)
                a = jnp.where(tril, a, 0.0).astype(BF16)
                vh = v_ref[0, rows, hcols]
                o = (jnp.dot(a, vh, preferred_element_type=F32)
                     + lax.dot_general(qm, st_b, _NT, preferred_element_type=F32))
                ut.append(lax.dot_general(vh, kd, _TN, preferred_element_type=F32))
                gate = _silu(r_ref[0, rows, hcols].astype(F32))
                o_ref[0, rows, hcols] = (_rms(o, gn_ref[...]) * gate).astype(BF16)
            st_ref[p] = st * dec[:, cols] + jnp.where(st_first, ut[0], ut[1])
        return carry

    lax.fori_loop(0, n_chunks, chunk_step, 0)


def _gla(proj, wg, bg, gn):
    B, S, _ = proj.shape
    T = T_GLA

    def col_block(width, off):
        return pl.BlockSpec((1, T, width), lambda b, i: (b, i, off // width))

    return pl.pallas_call(
        _gla_kernel,
        out_shape=jax.ShapeDtypeStruct((B, S, GLA_HEADS * GLA_DV), BF16),
        grid=(B, S // T),
        in_specs=[
            col_block(256, OFF_GQ), col_block(256, OFF_GK),
            col_block(512, OFF_GV), col_block(512, OFF_GR),
            col_block(LANES, OFF_GA),
            pl.BlockSpec((LANES, GLA_HEADS * GLA_DK), lambda b, i: (0, 0)),
            pl.BlockSpec((1, GLA_HEADS * GLA_DK), lambda b, i: (0, 0)),
            pl.BlockSpec((1, GLA_DV), lambda b, i: (0, 0)),
        ],
        out_specs=pl.BlockSpec((1, T, GLA_HEADS * GLA_DV), lambda b, i: (b, i, 0)),
        scratch_shapes=[
            pltpu.VMEM((GLA_HEADS // 2, GLA_DV, LANES), F32),
            pltpu.VMEM((T, GLA_HEADS * GLA_DK), BF16),
            pltpu.VMEM((T, GLA_HEADS * GLA_DK), BF16),
            pltpu.VMEM((T, GLA_HEADS * GLA_DK), BF16),
            pltpu.VMEM((T // GLA_CHUNK, 1, GLA_HEADS * GLA_DK), F32),
        ],
        compiler_params=_params("parallel", "arbitrary"),
        name="gla_chunked",
    )(proj, proj, proj, proj, proj, wg, bg, gn)


def _t5_bucket_np(rel):
    max_exact = REL_BUCKETS // 2
    n = np.maximum(rel, 0)
    nf = np.maximum(n, 1).astype(np.float32)
    large = max_exact + (np.log(nf / max_exact) / math.log(REL_MAX_DIST / max_exact)
                         * (REL_BUCKETS - max_exact)).astype(np.int32)
    large = np.minimum(large, REL_BUCKETS - 1)
    return np.where(n < max_exact, n, large).astype(np.int32)


def _bias_kernel(rb_ref, idx_ref, o_ref):
    h = pl.program_id(1)
    idx = idx_ref[0]
    far = rb_ref[REL_BUCKETS - 1, h]
    out = jnp.zeros(idx.shape, F32)
    for bkt in range(REL_BUCKETS - 1):
        out = jnp.where(idx == bkt, rb_ref[bkt, h] - far, out)
    out = jnp.where(idx < 0, MASK_VALUE, out * LOG2E)
    o_ref[0, 0, 0:TQ, :] = out
    o_ref[0, 0, TQ:2 * TQ, :] = out


def _bias_tiles(rel_bias):
    r = np.arange(TQ)[:, None]
    c = np.arange(TK)[None, :]
    idx_diag = np.where(r - c >= 0, _t5_bucket_np(r - c), -1)
    idx_prev = _t5_bucket_np(r - c + TK)
    idx = jnp.asarray(np.stack([idx_diag, idx_prev]).astype(np.int32))
    return pl.pallas_call(
        _bias_kernel,
        out_shape=jax.ShapeDtypeStruct((2, DIFF_HEADS, 2 * TQ, TK), F32),
        grid=(2, DIFF_HEADS),
        in_specs=[
            pl.BlockSpec(memory_space=pltpu.SMEM),
            pl.BlockSpec((1, TQ, TK), lambda t, h: (t, 0, 0)),
        ],
        out_specs=pl.BlockSpec((1, 1, 2 * TQ, TK), lambda t, h: (t, h, 0, 0)),
        compiler_params=_params("parallel", "parallel"),
        name="t5_bias_tiles",
    )(rel_bias.astype(F32), idx)


def _attn_kernel(dl_ref, q_ref, k_ref, v_ref, bias_ref, dn_ref, o_ref,
                 qs_ref, m_ref, l_ref, acc_ref, *, lam_init):
    qi = pl.program_id(2)
    q = q_ref[0]
    lane = lax.broadcasted_iota(jnp.int32, q.shape, 1)
    zero = jnp.zeros_like(q)
    qs_ref[0:TQ, :] = jnp.where(lane < DIFF_DQK, q, zero)
    qs_ref[TQ:2 * TQ, :] = jnp.where(lane >= DIFF_DQK, q, zero)
    m_ref[...] = jnp.full(m_ref.shape, MASK_VALUE, F32)
    l_ref[...] = jnp.zeros_like(l_ref)
    acc_ref[...] = jnp.zeros_like(acc_ref)

    def process(j, bias):
        rows = pl.ds(pl.multiple_of(j * TK, TK), TK)
        kt = k_ref[0, rows, :]
        vt = v_ref[0, rows, :]
        s = lax.dot_general(qs_ref[...], kt, _NT, preferred_element_type=F32)
        if bias is not None:
            s = s + bias
        m_prev = m_ref[...]
        m_next = jnp.maximum(m_prev, jnp.max(s, axis=1, keepdims=True))
        alpha = jnp.exp2(m_prev - m_next)
        p = jnp.exp2(s - jnp.concatenate([m_next] * (TK // LANES), axis=1))
        l_ref[...] = alpha * l_ref[...] + jnp.sum(p, axis=1, keepdims=True)
        acc_ref[...] = acc_ref[...] * alpha + jnp.dot(p.astype(BF16), vt, preferred_element_type=F32)
        m_ref[...] = m_next

    def bulk(j, carry):
        process(j, None)
        return carry

    lax.fori_loop(0, jnp.maximum(qi - 1, 0), bulk, 0)

    @pl.when(qi >= 1)
    def _():
        process(qi - 1, bias_ref[1, 0])

    process(qi, bias_ref[0, 0])

    dl = dl_ref[...]
    lam = (jnp.exp(jnp.sum(dl[0:1] * dl[1:2], axis=1, keepdims=True))
           - jnp.exp(jnp.sum(dl[2:3] * dl[3:4], axis=1, keepdims=True)) + lam_init)
    acc = acc_ref[...]
    l = l_ref[...]
    o = acc[0:TQ] / l[0:TQ] - lam * (acc[TQ:2 * TQ] / l[TQ:2 * TQ])
    o_ref[0] = (_rms(o, dn_ref[...]) * (1.0 - lam_init)).astype(BF16)


def _diff_attention(proj, bias, diff_lambda, diff_norm, lam_init):
    B, S, _ = proj.shape

    def head_block(rows, off):
        if rows == S:
            return pl.BlockSpec((1, S, LANES), lambda b, h, i: (b, 0, off // LANES + h))
        return pl.BlockSpec((1, rows, LANES), lambda b, h, i: (b, i, off // LANES + h))

    return pl.pallas_call(
        functools.partial(_attn_kernel, lam_init=lam_init),
        out_shape=jax.ShapeDtypeStruct((B, S, DIFF_HEADS * DIFF_DV), BF16),
        grid=(B, DIFF_HEADS, S // TQ),
        in_specs=[
            pl.BlockSpec((4, DIFF_DQK), lambda b, h, i: (0, 0)),
            head_block(TQ, OFF_DQ), head_block(S, OFF_DK), head_block(S, OFF_DV),
            pl.BlockSpec((2, 1, 2 * TQ, TK), lambda b, h, i: (0, h, 0, 0)),
            pl.BlockSpec((1, DIFF_DV), lambda b, h, i: (0, 0)),
        ],
        out_specs=pl.BlockSpec((1, TQ, DIFF_DV), lambda b, h, i: (b, i, h)),
        scratch_shapes=[
            pltpu.VMEM((2 * TQ, LANES), BF16),
            pltpu.VMEM((2 * TQ, LANES), F32),
            pltpu.VMEM((2 * TQ, LANES), F32),
            pltpu.VMEM((2 * TQ, DIFF_DV), F32),
        ],
        compiler_params=_params("parallel", "parallel", "arbitrary"),
        name="diff_attention",
    )(diff_lambda, proj, proj, proj, bias, diff_norm)


def _post_kernel(og_ref, od_ref, x_ref, g1_ref, sc2_ref, sh2_ref, g2_ref, n_ref,
                 wo_ref, wgu_ref, wd_ref, xo_ref):
    half = GLA_HEADS * GLA_DV
    y = (jnp.dot(og_ref[0], wo_ref[0:half, :], preferred_element_type=F32)
         + jnp.dot(od_ref[0], wo_ref[half:2 * half, :], preferred_element_type=F32))
    x1 = x_ref[0] + g1_ref[0] * _rms(y, n_ref[0:1, :])
    h = (_rms(x1, n_ref[1:2, :]) * (1.0 + sc2_ref[0]) + sh2_ref[0]).astype(BF16)
    acc = jnp.zeros(x1.shape, F32)
    for off, width in FF_CHUNKS:
        gate = jnp.dot(h, wgu_ref[:, off:off + width], preferred_element_type=F32)
        up = jnp.dot(h, wgu_ref[:, D_FF + off:D_FF + off + width], preferred_element_type=F32)
        act = (_silu(gate) * up).astype(BF16)
        acc = acc + jnp.dot(act, wd_ref[off:off + width, :], preferred_element_type=F32)
    xo_ref[0] = x1 + g2_ref[0] * _rms(acc, n_ref[2:3, :])


def _post(og, od, x, g1, sc2, sh2, g2, gains, wo, wgu, wd):
    B, S, D = x.shape
    tm = TM_POST
    vec = pl.BlockSpec((1, 1, D), lambda b, i: (b, 0, 0))

    def resident(shape):
        return pl.BlockSpec(shape, lambda b, i: (0,) * len(shape), pipeline_mode=pl.Buffered(1))

    return pl.pallas_call(
        _post_kernel,
        out_shape=jax.ShapeDtypeStruct((B, S, D), F32),
        grid=(B, S // tm),
        in_specs=[
            pl.BlockSpec((1, tm, og.shape[-1]), lambda b, i: (b, i, 0)),
            pl.BlockSpec((1, tm, od.shape[-1]), lambda b, i: (b, i, 0)),
            pl.BlockSpec((1, tm, D), lambda b, i: (b, i, 0)),
            vec, vec, vec, vec,
            pl.BlockSpec((3, D), lambda b, i: (0, 0)),
            resident(wo.shape), resident(wgu.shape), resident(wd.shape),
        ],
        out_specs=pl.BlockSpec((1, tm, D), lambda b, i: (b, i, 0)),
        compiler_params=_params("parallel", "parallel"),
        name="outproj_ffn",
    )(og, od, x, g1, sc2, sh2, g2, gains, wo, wgu, wd)


def kernel(x, c, w_mod, b_mod, norm_gains, w_in, w_gla_gate, b_gla_gate, gla_norm, diff_lambda,
           diff_norm, rel_bias, w_out, w_gate_up, w_down):
    B, S, D = x.shape
    assert D == D_MODEL and S % T_GLA == 0 and S % TQ == 0 and S % TM_PROJ == 0 and S % TM_POST == 0
    f32 = lambda a: a.astype(F32)

    n_g = 1536
    n_d = 1536
    w_in = f32(w_in)
    dq_scale = (DIFF_DQK ** -0.5) * LOG2E
    w_diff = w_in[:, :, n_g + GLA_LOWRANK:]
    w_diff = jnp.concatenate([w_diff[:, :, :512] * dq_scale, w_diff[:, :, 512:]], axis=-1)
    w_in_p = jnp.concatenate(
        [w_in[:, :, :n_g], w_diff, w_in[:, :, n_g:n_g + GLA_LOWRANK],
         jnp.zeros((DEPTH, D, PROJ_W - OFF_GA - GLA_LOWRANK), F32)], axis=-1).astype(BF16)
    assert w_in_p.shape[-1] == PROJ_W and n_g + n_d == OFF_GA
    wg_p = jnp.concatenate(
        [f32(w_gla_gate), jnp.zeros((DEPTH, LANES - GLA_LOWRANK, GLA_HEADS * GLA_DK), F32)],
        axis=1).astype(BF16)
    w_out_b = w_out.astype(BF16)
    w_gu_b = w_gate_up.astype(BF16)
    w_down_b = w_down.astype(BF16)

    mod = _modulation(f32(c), f32(w_mod), f32(b_mod))
    bias = _bias_tiles(rel_bias)

    x = f32(x)
    for l in range(DEPTH):
        lam_init = 0.8 - 0.6 * math.exp(-0.3 * l)
        m = mod[l].reshape(B, 6, 1, D)
        sh1, sc1, g1, sh2, sc2, g2 = (m[:, i] for i in range(6))
        gains = f32(norm_gains[l])
        proj = _inproj(x, sc1, sh1, gains[0:1], w_in_p[l])
        og = _gla(proj, wg_p[l], f32(b_gla_gate[l]).reshape(1, -1), f32(gla_norm[l]).reshape(1, -1))
        od = _diff_attention(proj, bias, f32(diff_lambda[l]), f32(diff_norm[l]).reshape(1, -1), lam_init)
        x = _post(og, od, x, g1, sc2, sh2, g2, gains[1:4], w_out_b[l], w_gu_b[l], w_down_b[l])
    return x
```

```python
import functools
import math

import numpy as np
import jax
import jax.numpy as jnp
from jax import lax
from jax.experimental import pallas as pl
from jax.experimental.pallas import tpu as pltpu

F32 = jnp.float32
BF16 = jnp.bfloat16

D_MODEL = 1024
DEPTH = 4
GLA_HEADS = 4
GLA_DV = 128
GLA_DK = 64
GLA_LOWRANK = 16
GLA_TAU = 16.0
GLA_CHUNK = 64
DIFF_HEADS = 4
DIFF_DV = 128
DIFF_DQK = 64
REL_BUCKETS = 32
REL_MAX_DIST = 128
D_FF = 2816
RMS_EPS = 1e-6
LOG2E = 1.4426950408889634

V7X_VMEM_BYTES = 64 * 1024 * 1024
VMEM_LIMIT_BYTES = V7X_VMEM_BYTES - 8 * 1024 * 1024
LANES = 128

OFF_GQ, OFF_GK, OFF_GV, OFF_GR = 0, 256, 512, 1024
OFF_DQ, OFF_DK, OFF_DV, OFF_GA = 1536, 2048, 2560, 3072
PROJ_W = 3200
PROJ_CHUNKS = ((0, 512), (512, 512), (1024, 512), (1536, 512), (2048, 512), (2560, 512), (3072, 128))

TM_PROJ = 512
TM_POST = 512
T_GLA = 512
TQ = 256
TK = 256
FF_CHUNKS = ((0, 512), (512, 512), (1024, 512), (1536, 512), (2048, 512), (2560, 256))
MASK_VALUE = -1e30

_NT = (((1,), (1,)), ((), ()))
_TN = (((0,), (0,)), ((), ()))


def _params(*sem):
    return pltpu.CompilerParams(dimension_semantics=sem, vmem_limit_bytes=VMEM_LIMIT_BYTES)


def _rms(x, g):
    ms = jnp.mean(x * x, axis=-1, keepdims=True)
    return x * lax.rsqrt(ms + RMS_EPS) * g


def _silu(x):
    return x * jax.nn.sigmoid(x)


def _mod_kernel(c_ref, w_ref, b_ref, o_ref):
    cond = _silu(c_ref[...]).astype(BF16)
    o_ref[0] = jnp.dot(cond, w_ref[0].astype(BF16), preferred_element_type=F32) + b_ref[0]


def _modulation(c, w_mod, b_mod):
    B = c.shape[0]
    n = w_mod.shape[-1]
    tn = n // 4
    return pl.pallas_call(
        _mod_kernel,
        out_shape=jax.ShapeDtypeStruct((DEPTH, B, n), F32),
        grid=(DEPTH, n // tn),
        in_specs=[
            pl.BlockSpec((B, D_MODEL), lambda l, j: (0, 0)),
            pl.BlockSpec((1, D_MODEL, tn), lambda l, j: (l, 0, j)),
            pl.BlockSpec((1, 1, tn), lambda l, j: (l, 0, j)),
        ],
        out_specs=pl.BlockSpec((1, B, tn), lambda l, j: (l, 0, j)),
        compiler_params=_params("parallel", "parallel"),
        name="adaln_modulation",
    )(c, w_mod, b_mod.reshape(DEPTH, 1, n))


def _inproj_kernel(x_ref, sc_ref, sh_ref, g_ref, w_ref, o_ref):
    h = _rms(x_ref[0], g_ref[...]) * (1.0 + sc_ref[0]) + sh_ref[0]
    hb = h.astype(BF16)
    for off, width in PROJ_CHUNKS:
        o_ref[0, :, off:off + width] = jnp.dot(
            hb, w_ref[:, off:off + width], preferred_element_type=F32).astype(BF16)


def _inproj(x, sc, sh, gain, w):
    B, S, D = x.shape
    vec = pl.BlockSpec((1, 1, D), lambda b, i: (b, 0, 0))
    return pl.pallas_call(
        _inproj_kernel,
        out_shape=jax.ShapeDtypeStruct((B, S, PROJ_W), BF16),
        grid=(B, S // TM_PROJ),
        in_specs=[
            pl.BlockSpec((1, TM_PROJ, D), lambda b, i: (b, i, 0)),
            vec, vec,
            pl.BlockSpec((1, D), lambda b, i: (0, 0)),
            pl.BlockSpec((D, PROJ_W), lambda b, i: (0, 0), pipeline_mode=pl.Buffered(1)),
        ],
        out_specs=pl.BlockSpec((1, TM_PROJ, PROJ_W), lambda b, i: (b, i, 0)),
        compiler_params=_params("parallel", "parallel"),
        name="in_projection",
    )(x, sc, sh, gain, w)


def _gla_kernel(q_ref, k_ref, v_ref, r_ref, a_ref, wg_ref, bg_ref, gn_ref, o_ref,
                st_ref, qe_ref, ke_ref, kd_ref, dec_ref):
    C = GLA_CHUNK
    n_chunks = T_GLA // C

    @pl.when(pl.program_id(1) == 0)
    def _():
        st_ref[...] = jnp.zeros_like(st_ref)

    z = jnp.dot(a_ref[0], wg_ref[...], preferred_element_type=F32) + bg_ref[...]
    la = (jnp.minimum(z, 0.0) - jnp.log1p(jnp.exp(-jnp.abs(z)))) * (1.0 / GLA_TAU)
    la_hi = la.astype(BF16)
    la_lo = (la - la_hi.astype(F32)).astype(BF16)
    row = lax.broadcasted_iota(jnp.int32, (C, C), 0)
    col = lax.broadcasted_iota(jnp.int32, (C, C), 1)
    tril = row >= col
    ltri = jnp.where(tril, 1.0, 0.0).astype(BF16)
    q = q_ref[0].astype(F32)
    k = k_ref[0].astype(F32)
    for c in range(n_chunks):
        sl = slice(c * C, (c + 1) * C)
        b = (jnp.dot(ltri, la_hi[sl], preferred_element_type=F32)
             + jnp.dot(ltri, la_lo[sl], preferred_element_type=F32))
        b_last = b[C - 1:C, :]
        qe_ref[sl, :] = (q[sl] * (GLA_DK ** -0.5) * jnp.exp(b)).astype(BF16)
        ke_ref[sl, :] = (k[sl] * jnp.exp(-b)).astype(BF16)
        kd_ref[sl, :] = (k[sl] * jnp.exp(b_last - b)).astype(BF16)
        dec_ref[c] = jnp.exp(b_last)

    lane = lax.broadcasted_iota(jnp.int32, (C, LANES), 1)
    first_head = lane < GLA_DK
    st_first = lax.broadcasted_iota(jnp.int32, (GLA_DV, LANES), 1) < GLA_DK

    def chunk_step(c, carry):
        r0 = pl.multiple_of(c * C, C)
        rows = pl.ds(r0, C)
        dec = dec_ref[c]
        for p in range(GLA_HEADS // 2):
            cols = slice(p * LANES, (p + 1) * LANES)
            qe = qe_ref[rows, cols]
            ke = ke_ref[rows, cols]
            kd = kd_ref[rows, cols]
            st = st_ref[p]
            st_b = st.astype(BF16)
            ut = []
            for j in range(2):
                hcols = slice((2 * p + j) * GLA_DV, (2 * p + j + 1) * GLA_DV)
                qm = jnp.where(first_head if j == 0 else ~first_head, qe, jnp.zeros_like(qe))
                a = lax.dot_general(qm, ke, _NT, preferred_element_type=F32)
                a = jnp.where(tril, a, 0.0).astype(BF16)
                vh = v_ref[0, rows, hcols]
                o = (jnp.dot(a, vh, preferred_element_type=F32)
                     + lax.dot_general(qm, st_b, _NT, preferred_element_type=F32))
                ut.append(lax.dot_general(vh, kd, _TN, preferred_element_type=F32))
                gate = _silu(r_ref[0, rows, hcols].astype(F32))
                o_ref[0, rows, hcols] = (_rms(o, gn_ref[...]) * gate).astype(BF16)
            st_ref[p] = st * dec[:, cols] + jnp.where(st_first, ut[0], ut[1])
        return carry

    lax.fori_loop(0, n_chunks, chunk_step, 0)


def _gla(proj, wg, bg, gn):
    B, S, _ = proj.shape
    T = T_GLA

    def col_block(width, off):
        return pl.BlockSpec((1, T, width), lambda b, i: (b, i, off // width))

    return pl.pallas_call(
        _gla_kernel,
        out_shape=jax.ShapeDtypeStruct((B, S, GLA_HEADS * GLA_DV), BF16),
        grid=(B, S // T),
        in_specs=[
            col_block(256, OFF_GQ), col_block(256, OFF_GK),
            col_block(512, OFF_GV), col_block(512, OFF_GR),
            col_block(LANES, OFF_GA),
            pl.BlockSpec((LANES, GLA_HEADS * GLA_DK), lambda b, i: (0, 0)),
            pl.BlockSpec((1, GLA_HEADS * GLA_DK), lambda b, i: (0, 0)),
            pl.BlockSpec((1, GLA_DV), lambda b, i: (0, 0)),
        ],
        out_specs=pl.BlockSpec((1, T, GLA_HEADS * GLA_DV), lambda b, i: (b, i, 0)),
        scratch_shapes=[
            pltpu.VMEM((GLA_HEADS // 2, GLA_DV, LANES), F32),
            pltpu.VMEM((T, GLA_HEADS * GLA_DK), BF16),
            pltpu.VMEM((T, GLA_HEADS * GLA_DK), BF16),
            pltpu.VMEM((T, GLA_HEADS * GLA_DK), BF16),
            pltpu.VMEM((T // GLA_CHUNK, 1, GLA_HEADS * GLA_DK), F32),
        ],
        compiler_params=_params("parallel", "arbitrary"),
        name="gla_chunked",
    )(proj, proj, proj, proj, proj, wg, bg, gn)


def _t5_bucket_np(rel):
    max_exact = REL_BUCKETS // 2
    n = np.maximum(rel, 0)
    nf = np.maximum(n, 1).astype(np.float32)
    large = max_exact + (np.log(nf / max_exact) / math.log(REL_MAX_DIST / max_exact)
                         * (REL_BUCKETS - max_exact)).astype(np.int32)
    large = np.minimum(large, REL_BUCKETS - 1)
    return np.where(n < max_exact, n, large).astype(np.int32)


def _bias_kernel(rb_ref, idx_ref, o_ref):
    h = pl.program_id(1)
    idx = idx_ref[0]
    far = rb_ref[REL_BUCKETS - 1, h]
    out = jnp.zeros(idx.shape, F32)
    for bkt in range(REL_BUCKETS - 1):
        out = jnp.where(idx == bkt, rb_ref[bkt, h] - far, out)
    out = jnp.where(idx < 0, MASK_VALUE, out * LOG2E)
    o_ref[0, 0, 0:TQ, :] = out
    o_ref[0, 0, TQ:2 * TQ, :] = out


def _bias_tiles(rel_bias):
    r = np.arange(TQ)[:, None]
    c = np.arange(TK)[None, :]
    idx_diag = np.where(r - c >= 0, _t5_bucket_np(r - c), -1)
    idx_prev = _t5_bucket_np(r - c + TK)
    idx = jnp.asarray(np.stack([idx_diag, idx_prev]).astype(np.int32))
    return pl.pallas_call(
        _bias_kernel,
        out_shape=jax.ShapeDtypeStruct((2, DIFF_HEADS, 2 * TQ, TK), F32),
        grid=(2, DIFF_HEADS),
        in_specs=[
            pl.BlockSpec(memory_space=pltpu.SMEM),
            pl.BlockSpec((1, TQ, TK), lambda t, h: (t, 0, 0)),
        ],
        out_specs=pl.BlockSpec((1, 1, 2 * TQ, TK), lambda t, h: (t, h, 0, 0)),
        compiler_params=_params("parallel", "parallel"),
        name="t5_bias_tiles",
    )(rel_bias.astype(F32), idx)


def _attn_kernel(dl_ref, q_ref, k_ref, v_ref, bias_ref, dn_ref, o_ref,
                 qs_ref, m_ref, l_ref, acc_ref, *, lam_init):
    qi = pl.program_id(1)
    lane = lax.broadcasted_iota(jnp.int32, (TQ, LANES), 1)
    for h in range(DIFF_HEADS):
        q = q_ref[0, :, h * LANES:(h + 1) * LANES]
        zero = jnp.zeros_like(q)
        qs_ref[h, 0:TQ, :] = jnp.where(lane < DIFF_DQK, q, zero)
        qs_ref[h, TQ:2 * TQ, :] = jnp.where(lane >= DIFF_DQK, q, zero)
    m_ref[...] = jnp.full(m_ref.shape, MASK_VALUE, F32)
    l_ref[...] = jnp.zeros_like(l_ref)
    acc_ref[...] = jnp.zeros_like(acc_ref)

    def process(j, bias_idx):
        rows = pl.ds(pl.multiple_of(j * TK, TK), TK)
        for h in range(DIFF_HEADS):
            cols = slice(h * LANES, (h + 1) * LANES)
            kt = k_ref[0, rows, cols]
            vt = v_ref[0, rows, cols]
            s = lax.dot_general(qs_ref[h], kt, _NT, preferred_element_type=F32)
            if bias_idx is not None:
                s = s + bias_ref[bias_idx, h]
            m_prev = m_ref[h]
            m_next = jnp.maximum(m_prev, jnp.max(s, axis=1, keepdims=True))
            alpha = jnp.exp2(m_prev - m_next)
            p = jnp.exp2(s - jnp.concatenate([m_next] * (TK // LANES), axis=1))
            l_ref[h] = alpha * l_ref[h] + sum(p[:, c * LANES:(c + 1) * LANES] for c in range(TK // LANES))
            acc_ref[h] = acc_ref[h] * alpha + jnp.dot(p.astype(BF16), vt, preferred_element_type=F32)
            m_ref[h] = m_next

    def bulk(j, carry):
        process(j, None)
        return carry

    lax.fori_loop(0, jnp.maximum(qi - 1, 0), bulk, 0)

    @pl.when(qi >= 1)
    def _():
        process(qi - 1, 1)

    process(qi, 0)

    dl = dl_ref[...]
    lam = (jnp.exp(jnp.sum(dl[0:1] * dl[1:2], axis=1, keepdims=True))
           - jnp.exp(jnp.sum(dl[2:3] * dl[3:4], axis=1, keepdims=True)) + lam_init)
    for h in range(DIFF_HEADS):
        acc = acc_ref[h]
        l = jnp.sum(l_ref[h], axis=1, keepdims=True)
        o = acc[0:TQ] / l[0:TQ] - lam * (acc[TQ:2 * TQ] / l[TQ:2 * TQ])
        o_ref[0, :, h * DIFF_DV:(h + 1) * DIFF_DV] = (
            _rms(o, dn_ref[...]) * (1.0 - lam_init)).astype(BF16)


def _diff_attention(proj, bias, diff_lambda, diff_norm, lam_init):
    B, S, _ = proj.shape
    width = DIFF_HEADS * LANES
    return pl.pallas_call(
        functools.partial(_attn_kernel, lam_init=lam_init),
        out_shape=jax.ShapeDtypeStruct((B, S, DIFF_HEADS * DIFF_DV), BF16),
        grid=(B, S // TQ),
        in_specs=[
            pl.BlockSpec((4, DIFF_DQK), lambda b, i: (0, 0)),
            pl.BlockSpec((1, TQ, width), lambda b, i: (b, i, OFF_DQ // width)),
            pl.BlockSpec((1, S, width), lambda b, i: (b, 0, OFF_DK // width)),
            pl.BlockSpec((1, S, width), lambda b, i: (b, 0, OFF_DV // width)),
            pl.BlockSpec(bias.shape, lambda b, i: (0, 0, 0, 0), pipeline_mode=pl.Buffered(1)),
            pl.BlockSpec((1, DIFF_DV), lambda b, i: (0, 0)),
        ],
        out_specs=pl.BlockSpec((1, TQ, DIFF_HEADS * DIFF_DV), lambda b, i: (b, i, 0)),
        scratch_shapes=[
            pltpu.VMEM((DIFF_HEADS, 2 * TQ, LANES), BF16),
            pltpu.VMEM((DIFF_HEADS, 2 * TQ, LANES), F32),
            pltpu.VMEM((DIFF_HEADS, 2 * TQ, LANES), F32),
            pltpu.VMEM((DIFF_HEADS, 2 * TQ, DIFF_DV), F32),
        ],
        compiler_params=_params("parallel", "arbitrary"),
        name="diff_attention",
    )(diff_lambda, proj, proj, proj, bias, diff_norm)


def _post_kernel(og_ref, od_ref, x_ref, g1_ref, sc2_ref, sh2_ref, g2_ref, n_ref,
                 wo_ref, wgu_ref, wd_ref, xo_ref):
    half = GLA_HEADS * GLA_DV
    y = (jnp.dot(og_ref[0], wo_ref[0:half, :], preferred_element_type=F32)
         + jnp.dot(od_ref[0], wo_ref[half:2 * half, :], preferred_element_type=F32))
    x1 = x_ref[0] + g1_ref[0] * _rms(y, n_ref[0:1, :])
    h = (_rms(x1, n_ref[1:2, :]) * (1.0 + sc2_ref[0]) + sh2_ref[0]).astype(BF16)
    acc = jnp.zeros(x1.shape, F32)
    for off, width in FF_CHUNKS:
        gate = jnp.dot(h, wgu_ref[:, off:off + width], preferred_element_type=F32)
        up = jnp.dot(h, wgu_ref[:, D_FF + off:D_FF + off + width], preferred_element_type=F32)
        act = (_silu(gate) * up).astype(BF16)
        acc = acc + jnp.dot(act, wd_ref[off:off + width, :], preferred_element_type=F32)
    xo_ref[0] = x1 + g2_ref[0] * _rms(acc, n_ref[2:3, :])


def _post(og, od, x, g1, sc2, sh2, g2, gains, wo, wgu, wd):
    B, S, D = x.shape
    tm = TM_POST
    vec = pl.BlockSpec((1, 1, D), lambda b, i: (b, 0, 0))

    def resident(shape):
        return pl.BlockSpec(shape, lambda b, i: (0,) * len(shape), pipeline_mode=pl.Buffered(1))

    return pl.pallas_call(
        _post_kernel,
        out_shape=jax.ShapeDtypeStruct((B, S, D), F32),
        grid=(B, S // tm),
        in_specs=[
            pl.BlockSpec((1, tm, og.shape[-1]), lambda b, i: (b, i, 0)),
            pl.BlockSpec((1, tm, od.shape[-1]), lambda b, i: (b, i, 0)),
            pl.BlockSpec((1, tm, D), lambda b, i: (b, i, 0)),
            vec, vec, vec, vec,
            pl.BlockSpec((3, D), lambda b, i: (0, 0)),
            resident(wo.shape), resident(wgu.shape), resident(wd.shape),
        ],
        out_specs=pl.BlockSpec((1, tm, D), lambda b, i: (b, i, 0)),
        compiler_params=_params("parallel", "parallel"),
        name="outproj_ffn",
    )(og, od, x, g1, sc2, sh2, g2, gains, wo, wgu, wd)


def kernel(x, c, w_mod, b_mod, norm_gains, w_in, w_gla_gate, b_gla_gate, gla_norm, diff_lambda,
           diff_norm, rel_bias, w_out, w_gate_up, w_down):
    B, S, D = x.shape
    assert D == D_MODEL and S % T_GLA == 0 and S % TQ == 0 and S % TM_PROJ == 0 and S % TM_POST == 0
    f32 = lambda a: a.astype(F32)

    n_g = 1536
    n_d = 1536
    w_in = f32(w_in)
    dq_scale = (DIFF_DQK ** -0.5) * LOG2E
    w_diff = w_in[:, :, n_g + GLA_LOWRANK:]
    w_diff = jnp.concatenate([w_diff[:, :, :512] * dq_scale, w_diff[:, :, 512:]], axis=-1)
    w_in_p = jnp.concatenate(
        [w_in[:, :, :n_g], w_diff, w_in[:, :, n_g:n_g + GLA_LOWRANK],
         jnp.zeros((DEPTH, D, PROJ_W - OFF_GA - GLA_LOWRANK), F32)], axis=-1).astype(BF16)
    assert w_in_p.shape[-1] == PROJ_W and n_g + n_d == OFF_GA
    wg_p = jnp.concatenate(
        [f32(w_gla_gate), jnp.zeros((DEPTH, LANES - GLA_LOWRANK, GLA_HEADS * GLA_DK), F32)],
        axis=1).astype(BF16)
    w_out_b = w_out.astype(BF16)
    w_gu_b = w_gate_up.astype(BF16)
    w_down_b = w_down.astype(BF16)

    mod = _modulation(f32(c), f32(w_mod), f32(b_mod))
    bias = _bias_tiles(rel_bias)

    x = f32(x)
    for l in range(DEPTH):
        lam_init = 0.8 - 0.6 * math.exp(-0.3 * l)
        m = mod[l].reshape(B, 6, 1, D)
        sh1, sc1, g1, sh2, sc2, g2 = (m[:, i] for i in range(6))
        gains = f32(norm_gains[l])
        proj = _inproj(x, sc1, sh1, gains[0:1], w_in_p[l])
        og = _gla(proj, wg_p[l], f32(b_gla_gate[l]).reshape(1, -1), f32(gla_norm[l]).reshape(1, -1))
        od = _diff_attention(proj, bias, f32(diff_lambda[l]), f32(diff_norm[l]).reshape(1, -1), lam_init)
        x = _post(og, od, x, g1, sc2, sh2, g2, gains[1:4], w_out_b[l], w_gu_b[l], w_down_b[l])
    return x
```

```python
import functools
import math

import numpy as np
import jax
import jax.numpy as jnp
from jax import lax
from jax.experimental import pallas as pl
from jax.experimental.pallas import tpu as pltpu

F32 = jnp.float32
BF16 = jnp.bfloat16

D_MODEL = 1024
DEPTH = 4
GLA_HEADS = 4
GLA_DV = 128
GLA_DK = 64
GLA_LOWRANK = 16
GLA_TAU = 16.0
GLA_CHUNK = 64
DIFF_HEADS = 4
DIFF_DV = 128
DIFF_DQK = 64
REL_BUCKETS = 32
REL_MAX_DIST = 128
D_FF = 2816
RMS_EPS = 1e-6
LOG2E = 1.4426950408889634

V7X_VMEM_BYTES = 64 * 1024 * 1024
VMEM_LIMIT_BYTES = V7X_VMEM_BYTES - 8 * 1024 * 1024
LANES = 128

OFF_GQ, OFF_GK, OFF_GV, OFF_GR = 0, 256, 512, 1024
OFF_DQ, OFF_DK, OFF_GA = 1536, 2048, 2560
PROJ_W = 2688
PROJ_CHUNKS = ((0, 512), (512, 512), (1024, 512), (1536, 512), (2048, 512), (2560, 128))

TM_PROJ = 512
TM_POST = 512
T_GLA = 512
TQ = 256
TK = 256
NQ = 2 * TQ
VT_ROWS = DIFF_DV + 16
FF_CHUNKS = ((0, 512), (512, 512), (1024, 512), (1536, 512), (2048, 512), (2560, 256))
MASK_VALUE = -1e30

_NT = (((1,), (1,)), ((), ()))
_TN = (((0,), (0,)), ((), ()))


def _params(*sem):
    return pltpu.CompilerParams(dimension_semantics=sem, vmem_limit_bytes=VMEM_LIMIT_BYTES)


def _rms(x, g):
    ms = jnp.mean(x * x, axis=-1, keepdims=True)
    return x * lax.rsqrt(ms + RMS_EPS) * g


def _silu(x):
    return x * jax.nn.sigmoid(x)


def _mod_kernel(c_ref, w_ref, b_ref, o_ref):
    cond = _silu(c_ref[...]).astype(BF16)
    o_ref[0] = jnp.dot(cond, w_ref[0].astype(BF16), preferred_element_type=F32) + b_ref[0]


def _modulation(c, w_mod, b_mod):
    B = c.shape[0]
    n = w_mod.shape[-1]
    tn = n // 4
    return pl.pallas_call(
        _mod_kernel,
        out_shape=jax.ShapeDtypeStruct((DEPTH, B, n), F32),
        grid=(DEPTH, n // tn),
        in_specs=[
            pl.BlockSpec((B, D_MODEL), lambda l, j: (0, 0)),
            pl.BlockSpec((1, D_MODEL, tn), lambda l, j: (l, 0, j)),
            pl.BlockSpec((1, 1, tn), lambda l, j: (l, 0, j)),
        ],
        out_specs=pl.BlockSpec((1, B, tn), lambda l, j: (l, 0, j)),
        compiler_params=_params("parallel", "parallel"),
        name="adaln_modulation",
    )(c, w_mod, b_mod.reshape(DEPTH, 1, n))


def _inproj_kernel(x_ref, sc_ref, sh_ref, g_ref, w_ref, wvt_ref, o_ref, vt_ref):
    h = _rms(x_ref[0], g_ref[...]) * (1.0 + sc_ref[0]) + sh_ref[0]
    hb = h.astype(BF16)
    for off, width in PROJ_CHUNKS:
        o_ref[0, :, off:off + width] = jnp.dot(
            hb, w_ref[:, off:off + width], preferred_element_type=F32).astype(BF16)
    vt = lax.dot_general(wvt_ref[...], hb, _NT, preferred_element_type=F32)
    ones = jnp.ones((VT_ROWS - DIFF_DV, TK), BF16)
    for t in range(TM_PROJ // TK):
        for h in range(DIFF_HEADS):
            vt_ref[0, t, h * VT_ROWS:h * VT_ROWS + DIFF_DV, :] = vt[
                h * DIFF_DV:(h + 1) * DIFF_DV, t * TK:(t + 1) * TK].astype(BF16)
            vt_ref[0, t, h * VT_ROWS + DIFF_DV:(h + 1) * VT_ROWS, :] = ones


def _inproj(x, sc, sh, gain, w, wvt):
    B, S, D = x.shape
    vec = pl.BlockSpec((1, 1, D), lambda b, i: (b, 0, 0))
    nv = DIFF_HEADS * VT_ROWS
    return pl.pallas_call(
        _inproj_kernel,
        out_shape=(jax.ShapeDtypeStruct((B, S, PROJ_W), BF16),
                   jax.ShapeDtypeStruct((B, S // TK, nv, TK), BF16)),
        grid=(B, S // TM_PROJ),
        in_specs=[
            pl.BlockSpec((1, TM_PROJ, D), lambda b, i: (b, i, 0)),
            vec, vec,
            pl.BlockSpec((1, D), lambda b, i: (0, 0)),
            pl.BlockSpec((D, PROJ_W), lambda b, i: (0, 0), pipeline_mode=pl.Buffered(1)),
            pl.BlockSpec(wvt.shape, lambda b, i: (0, 0), pipeline_mode=pl.Buffered(1)),
        ],
        out_specs=(pl.BlockSpec((1, TM_PROJ, PROJ_W), lambda b, i: (b, i, 0)),
                   pl.BlockSpec((1, TM_PROJ // TK, nv, TK), lambda b, i: (b, i, 0, 0))),
        compiler_params=_params("parallel", "parallel"),
        name="in_projection",
    )(x, sc, sh, gain, w, wvt)


def _gla_kernel(q_ref, k_ref, v_ref, r_ref, a_ref, wg_ref, bg_ref, gn_ref, o_ref,
                st_ref, qe_ref, ke_ref, kd_ref):
    C = GLA_CHUNK
    n_chunks = T_GLA // C
    n_pairs = GLA_HEADS // 2

    @pl.when(pl.program_id(1) == 0)
    def _():
        st_ref[...] = jnp.zeros_like(st_ref)

    z = jnp.dot(a_ref[0], wg_ref[...], preferred_element_type=F32) + bg_ref[...]
    la = (jnp.minimum(z, 0.0) - jnp.log(1.0 + jnp.exp(-jnp.abs(z)))) * (1.0 / GLA_TAU)
    la_hi = la.astype(BF16)
    la_lo = (la - la_hi.astype(F32)).astype(BF16)
    row = lax.broadcasted_iota(jnp.int32, (C, C), 0)
    col = lax.broadcasted_iota(jnp.int32, (C, C), 1)
    ltri = jnp.where(row >= col, 1.0, 0.0).astype(BF16)
    q = q_ref[0].astype(F32)
    k = k_ref[0].astype(F32)
    decay = []
    for c in range(n_chunks):
        sl = slice(c * C, (c + 1) * C)
        b = (jnp.dot(ltri, la_hi[sl], preferred_element_type=F32)
             + jnp.dot(ltri, la_lo[sl], preferred_element_type=F32))
        b_last = b[C - 1:C, :]
        qe_ref[sl, :] = (q[sl] * (GLA_DK ** -0.5) * jnp.exp(b)).astype(BF16)
        ke_ref[sl, :] = (k[sl] * jnp.exp(-b)).astype(BF16)
        kd_ref[sl, :] = (k[sl] * jnp.exp(b_last - b)).astype(BF16)
        decay.append(jnp.exp(b_last))

    first_head = lax.broadcasted_iota(jnp.int32, (C, LANES), 1) < GLA_DK
    st_first = lax.broadcasted_iota(jnp.int32, (GLA_DV, LANES), 1) < GLA_DK
    ri = lax.broadcasted_iota(jnp.int32, (2 * C, 2 * C), 0)
    ci = lax.broadcasted_iota(jnp.int32, (2 * C, 2 * C), 1)
    pair_mask = jnp.logical_and((ri >= C) == (ci >= C), ri >= ci)

    states = [st_ref[p] for p in range(n_pairs)]
    for c in range(n_chunks):
        rows = slice(c * C, (c + 1) * C)
        for p in range(n_pairs):
            cols = slice(p * LANES, (p + 1) * LANES)
            pair = slice(2 * p * GLA_DV, (2 * p + 2) * GLA_DV)
            qe = qe_ref[rows, cols]
            ke = ke_ref[rows, cols]
            kd = kd_ref[rows, cols]
            zero = jnp.zeros_like(qe)
            qm = jnp.concatenate([jnp.where(first_head, qe, zero), jnp.where(first_head, zero, qe)], axis=0)
            a = lax.dot_general(qm, jnp.concatenate([ke, ke], axis=0), _NT, preferred_element_type=F32)
            a = jnp.where(pair_mask, a, 0.0).astype(BF16)
            v_pair = v_ref[0, rows, pair]
            v_stack = jnp.concatenate([v_pair[:, :GLA_DV], v_pair[:, GLA_DV:]], axis=0)
            st = states[p]
            o = (jnp.dot(a, v_stack, preferred_element_type=F32)
                 + lax.dot_general(qm, st.astype(BF16), _NT, preferred_element_type=F32))
            ut = lax.dot_general(v_pair, kd, _TN, preferred_element_type=F32)
            states[p] = st * decay[c][:, cols] + jnp.where(st_first, ut[:GLA_DV], ut[GLA_DV:])
            r_pair = r_ref[0, rows, pair]
            gate = _silu(jnp.concatenate([r_pair[:, :GLA_DV], r_pair[:, GLA_DV:]], axis=0).astype(F32))
            on = (_rms(o, gn_ref[...]) * gate).astype(BF16)
            o_ref[0, rows, 2 * p * GLA_DV:(2 * p + 1) * GLA_DV] = on[:C]
            o_ref[0, rows, (2 * p + 1) * GLA_DV:(2 * p + 2) * GLA_DV] = on[C:]
    for p in range(n_pairs):
        st_ref[p] = states[p]


def _gla(proj, wg, bg, gn):
    B, S, _ = proj.shape
    T = T_GLA

    def col_block(width, off):
        return pl.BlockSpec((1, T, width), lambda b, i: (b, i, off // width))

    return pl.pallas_call(
        _gla_kernel,
        out_shape=jax.ShapeDtypeStruct((B, S, GLA_HEADS * GLA_DV), BF16),
        grid=(B, S // T),
        in_specs=[
            col_block(256, OFF_GQ), col_block(256, OFF_GK),
            col_block(512, OFF_GV), col_block(512, OFF_GR),
            col_block(LANES, OFF_GA),
            pl.BlockSpec((LANES, GLA_HEADS * GLA_DK), lambda b, i: (0, 0)),
            pl.BlockSpec((1, GLA_HEADS * GLA_DK), lambda b, i: (0, 0)),
            pl.BlockSpec((1, GLA_DV), lambda b, i: (0, 0)),
        ],
        out_specs=pl.BlockSpec((1, T, GLA_HEADS * GLA_DV), lambda b, i: (b, i, 0)),
        scratch_shapes=[
            pltpu.VMEM((GLA_HEADS // 2, GLA_DV, LANES), F32),
            pltpu.VMEM((T, GLA_HEADS * GLA_DK), BF16),
            pltpu.VMEM((T, GLA_HEADS * GLA_DK), BF16),
            pltpu.VMEM((T, GLA_HEADS * GLA_DK), BF16),
        ],
        compiler_params=_params("parallel", "arbitrary"),
        name="gla_chunked",
    )(proj, proj, proj, proj, proj, wg, bg, gn)


def _t5_bucket_np(rel):
    max_exact = REL_BUCKETS // 2
    n = np.maximum(rel, 0)
    nf = np.maximum(n, 1).astype(np.float32)
    large = max_exact + (np.log(nf / max_exact) / math.log(REL_MAX_DIST / max_exact)
                         * (REL_BUCKETS - max_exact)).astype(np.int32)
    large = np.minimum(large, REL_BUCKETS - 1)
    return np.where(n < max_exact, n, large).astype(np.int32)


def _bias_kernel(rb_ref, idx_ref, o_ref):
    h = pl.program_id(1)
    idx = idx_ref[0]
    far = rb_ref[REL_BUCKETS - 1, h]
    out = jnp.zeros(idx.shape, F32)
    for bkt in range(REL_BUCKETS - 1):
        out = jnp.where(idx == bkt, rb_ref[bkt, h] - far, out)
    out = jnp.where(idx < 0, MASK_VALUE, out * LOG2E)
    o_ref[0, 0, :, 0:TQ] = out
    o_ref[0, 0, :, TQ:NQ] = out


def _bias_tiles(rel_bias):
    key = np.arange(TK)[:, None]
    qry = np.arange(TQ)[None, :]
    idx_diag = np.where(qry - key >= 0, _t5_bucket_np(qry - key), -1)
    idx_prev = _t5_bucket_np(qry - key + TK)
    idx = jnp.asarray(np.stack([idx_diag, idx_prev]).astype(np.int32))
    return pl.pallas_call(
        _bias_kernel,
        out_shape=jax.ShapeDtypeStruct((2, DIFF_HEADS, TK, NQ), F32),
        grid=(2, DIFF_HEADS),
        in_specs=[
            pl.BlockSpec(memory_space=pltpu.SMEM),
            pl.BlockSpec((1, TK, TQ), lambda t, h: (t, 0, 0)),
        ],
        out_specs=pl.BlockSpec((1, 1, TK, NQ), lambda t, h: (t, h, 0, 0)),
        compiler_params=_params("parallel", "parallel"),
        name="t5_bias_tiles",
    )(rel_bias.astype(F32), idx)


def _attn_kernel(dl_ref, q_ref, k_ref, vt_ref, bias_ref, dn_ref, o_ref,
                 qs_ref, sa_ref, sb_ref, m_ref, acc_ref, *, lam_init):
    qi = pl.program_id(1)
    heads = range(DIFF_HEADS)
    hs = [slice(h * LANES, (h + 1) * LANES) for h in heads]
    lane = lax.broadcasted_iota(jnp.int32, (TQ, LANES), 1)
    for h in heads:
        q = q_ref[0, :, hs[h]]
        zero = jnp.zeros_like(q)
        qs_ref[h, 0:TQ, :] = jnp.where(lane < DIFF_DQK, q, zero)
        qs_ref[h, TQ:NQ, :] = jnp.where(lane >= DIFF_DQK, q, zero)
    m_ref[...] = jnp.full(m_ref.shape, MASK_VALUE, F32)
    acc_ref[...] = jnp.zeros_like(acc_ref)

    def scores(j, dst_ref):
        rows = pl.ds(pl.multiple_of(j * TK, TK), TK)
        for h in heads:
            dst_ref[h] = lax.dot_general(k_ref[0, rows, hs[h]], qs_ref[h], _NT,
                                         preferred_element_type=F32)

    def consume(j, src_ref, bias_idx):
        p, alpha = [], []
        for h in heads:
            s = src_ref[h] if bias_idx is None else src_ref[h] + bias_ref[bias_idx, h]
            m_prev = m_ref[h]
            m_next = jnp.maximum(m_prev, jnp.max(s, axis=0, keepdims=True))
            m_ref[h] = m_next
            alpha.append(jnp.exp2(m_prev - m_next))
            p.append(jnp.exp2(s - m_next).astype(BF16))
        for h in heads:
            acc_ref[h] = acc_ref[h] * alpha[h] + jnp.dot(
                vt_ref[0, j, h * VT_ROWS:(h + 1) * VT_ROWS, :], p[h], preferred_element_type=F32)

    n_plain = jnp.maximum(qi - 1, 0)
    scores(0, sa_ref)

    def pair(i, carry):
        j = 2 * i
        scores(j + 1, sb_ref)
        consume(j, sa_ref, None)
        scores(j + 2, sa_ref)
        consume(j + 1, sb_ref, None)
        return carry

    lax.fori_loop(0, n_plain // 2, pair, 0)
    odd = (n_plain % 2) == 1

    @pl.when(qi == 0)
    def _():
        consume(0, sa_ref, 0)

    @pl.when(jnp.logical_and(qi >= 1, jnp.logical_not(odd)))
    def _():
        scores(qi, sb_ref)
        consume(qi - 1, sa_ref, 1)
        consume(qi, sb_ref, 0)

    @pl.when(odd)
    def _():
        scores(qi - 1, sb_ref)
        consume(qi - 2, sa_ref, None)
        scores(qi, sa_ref)
        consume(qi - 1, sb_ref, 1)
        consume(qi, sa_ref, 0)

    dl = dl_ref[...]
    lam = (jnp.exp(jnp.sum(dl[0:1] * dl[1:2], axis=1, keepdims=True))
           - jnp.exp(jnp.sum(dl[2:3] * dl[3:4], axis=1, keepdims=True)) + lam_init)
    for h in heads:
        acc = acc_ref[h, 0:DIFF_DV, :]
        inv = 1.0 / acc_ref[h, DIFF_DV:DIFF_DV + 1, :]
        o = acc[:, 0:TQ] * inv[:, 0:TQ] - lam * (acc[:, TQ:NQ] * inv[:, TQ:NQ])
        ms = jnp.mean(o * o, axis=0, keepdims=True)
        on = o * lax.rsqrt(ms + RMS_EPS) * (dn_ref[...] * (1.0 - lam_init))
        o_ref[0, :, h * DIFF_DV:(h + 1) * DIFF_DV] = on.T.astype(BF16)


def _diff_attention(proj, vt, bias, diff_lambda, diff_norm_col, lam_init):
    B, S, _ = proj.shape
    width = DIFF_HEADS * LANES
    return pl.pallas_call(
        functools.partial(_attn_kernel, lam_init=lam_init),
        out_shape=jax.ShapeDtypeStruct((B, S, DIFF_HEADS * DIFF_DV), BF16),
        grid=(B, S // TQ),
        in_specs=[
            pl.BlockSpec((4, DIFF_DQK), lambda b, i: (0, 0)),
            pl.BlockSpec((1, TQ, width), lambda b, i: (b, i, OFF_DQ // width)),
            pl.BlockSpec((1, S, width), lambda b, i: (b, 0, OFF_DK // width)),
            pl.BlockSpec((1,) + vt.shape[1:], lambda b, i: (b, 0, 0, 0)),
            pl.BlockSpec(bias.shape, lambda b, i: (0, 0, 0, 0), pipeline_mode=pl.Buffered(1)),
            pl.BlockSpec((DIFF_DV, 1), lambda b, i: (0, 0)),
        ],
        out_specs=pl.BlockSpec((1, TQ, DIFF_HEADS * DIFF_DV), lambda b, i: (b, i, 0)),
        scratch_shapes=[
            pltpu.VMEM((DIFF_HEADS, NQ, LANES), BF16),
            pltpu.VMEM((DIFF_HEADS, TK, NQ), F32),
            pltpu.VMEM((DIFF_HEADS, TK, NQ), F32),
            pltpu.VMEM((DIFF_HEADS, 1, NQ), F32),
            pltpu.VMEM((DIFF_HEADS, VT_ROWS, NQ), F32),
        ],
        compiler_params=_params("parallel", "arbitrary"),
        name="diff_attention",
    )(diff_lambda, proj, proj, vt, bias, diff_norm_col)


def _post_kernel(og_ref, od_ref, x_ref, g1_ref, sc2_ref, sh2_ref, g2_ref, n_ref,
                 wo_ref, wgu_ref, wd_ref, xo_ref):
    half = GLA_HEADS * GLA_DV
    y = (jnp.dot(og_ref[0], wo_ref[0:half, :], preferred_element_type=F32)
         + jnp.dot(od_ref[0], wo_ref[half:2 * half, :], preferred_element_type=F32))
    x1 = x_ref[0] + g1_ref[0] * _rms(y, n_ref[0:1, :])
    h = (_rms(x1, n_ref[1:2, :]) * (1.0 + sc2_ref[0]) + sh2_ref[0]).astype(BF16)
    acc = jnp.zeros(x1.shape, F32)
    for off, width in FF_CHUNKS:
        gate = jnp.dot(h, wgu_ref[:, off:off + width], preferred_element_type=F32)
        up = jnp.dot(h, wgu_ref[:, D_FF + off:D_FF + off + width], preferred_element_type=F32)
        act = (_silu(gate) * up).astype(BF16)
        acc = acc + jnp.dot(act, wd_ref[off:off + width, :], preferred_element_type=F32)
    xo_ref[0] = x1 + g2_ref[0] * _rms(acc, n_ref[2:3, :])


def _post(og, od, x, g1, sc2, sh2, g2, gains, wo, wgu, wd):
    B, S, D = x.shape
    tm = TM_POST
    vec = pl.BlockSpec((1, 1, D), lambda b, i: (b, 0, 0))

    def resident(shape):
        return pl.BlockSpec(shape, lambda b, i: (0,) * len(shape), pipeline_mode=pl.Buffered(1))

    return pl.pallas_call(
        _post_kernel,
        out_shape=jax.ShapeDtypeStruct((B, S, D), F32),
        grid=(B, S // tm),
        in_specs=[
            pl.BlockSpec((1, tm, og.shape[-1]), lambda b, i: (b, i, 0)),
            pl.BlockSpec((1, tm, od.shape[-1]), lambda b, i: (b, i, 0)),
            pl.BlockSpec((1, tm, D), lambda b, i: (b, i, 0)),
            vec, vec, vec, vec,
            pl.BlockSpec((3, D), lambda b, i: (0, 0)),
            resident(wo.shape), resident(wgu.shape), resident(wd.shape),
        ],
        out_specs=pl.BlockSpec((1, tm, D), lambda b, i: (b, i, 0)),
        compiler_params=_params("parallel", "parallel"),
        name="outproj_ffn",
    )(og, od, x, g1, sc2, sh2, g2, gains, wo, wgu, wd)


def kernel(x, c, w_mod, b_mod, norm_gains, w_in, w_gla_gate, b_gla_gate, gla_norm, diff_lambda,
           diff_norm, rel_bias, w_out, w_gate_up, w_down):
    B, S, D = x.shape
    assert D == D_MODEL and S % T_GLA == 0 and S % TQ == 0 and S % TM_PROJ == 0 and S % TM_POST == 0
    f32 = lambda a: a.astype(F32)

    n_g = OFF_DQ
    w_in = f32(w_in)
    dq_scale = (DIFF_DQK ** -0.5) * LOG2E
    w_diff = w_in[:, :, n_g + GLA_LOWRANK:]
    w_in_p = jnp.concatenate(
        [w_in[:, :, :n_g], w_diff[:, :, :512] * dq_scale, w_diff[:, :, 512:1024],
         w_in[:, :, n_g:n_g + GLA_LOWRANK],
         jnp.zeros((DEPTH, D, PROJ_W - OFF_GA - GLA_LOWRANK), F32)], axis=-1).astype(BF16)
    assert w_in_p.shape[-1] == PROJ_W
    w_dvt = jnp.swapaxes(w_diff[:, :, 1024:], 1, 2).astype(BF16)
    wg_p = jnp.concatenate(
        [f32(w_gla_gate), jnp.zeros((DEPTH, LANES - GLA_LOWRANK, GLA_HEADS * GLA_DK), F32)],
        axis=1).astype(BF16)
    w_out_b = w_out.astype(BF16)
    w_gu_b = w_gate_up.astype(BF16)
    w_down_b = w_down.astype(BF16)

    mod = _modulation(f32(c), f32(w_mod), f32(b_mod))
    bias = _bias_tiles(rel_bias)

    x = f32(x)
    for l in range(DEPTH):
        lam_init = 0.8 - 0.6 * math.exp(-0.3 * l)
        m = mod[l].reshape(B, 6, 1, D)
        sh1, sc1, g1, sh2, sc2, g2 = (m[:, i] for i in range(6))
        gains = f32(norm_gains[l])
        proj, vt = _inproj(x, sc1, sh1, gains[0:1], w_in_p[l], w_dvt[l])
        og = _gla(proj, wg_p[l], f32(b_gla_gate[l]).reshape(1, -1), f32(gla_norm[l]).reshape(1, -1))
        od = _diff_attention(proj, vt, bias, f32(diff_lambda[l]), f32(diff_norm[l]).reshape(-1, 1), lam_init)
        x = _post(og, od, x, g1, sc2, sh2, g2, gains[1:4], w_out_b[l], w_gu_b[l], w_down_b[l])
    return x
```

```python
import functools
import math

import numpy as np
import jax
import jax.numpy as jnp
from jax import lax
from jax.experimental import pallas as pl
from jax.experimental.pallas import tpu as pltpu

F32 = jnp.float32
BF16 = jnp.bfloat16

D_MODEL = 1024
DEPTH = 4
GLA_HEADS = 4
GLA_DV = 128
GLA_DK = 64
GLA_LOWRANK = 16
GLA_TAU = 16.0
GLA_CHUNK = 64
DIFF_HEADS = 4
DIFF_DV = 128
DIFF_DQK = 64
REL_BUCKETS = 32
REL_MAX_DIST = 128
D_FF = 2816
RMS_EPS = 1e-6
LOG2E = 1.4426950408889634

V7X_VMEM_BYTES = 64 * 1024 * 1024
VMEM_LIMIT_BYTES = V7X_VMEM_BYTES - 8 * 1024 * 1024
LANES = 128

OFF_GQ, OFF_GK, OFF_GV, OFF_GR = 0, 256, 512, 1024
OFF_DQ, OFF_DK, OFF_GA = 1536, 2048, 2560
PROJ_W = 2688
PROJ_CHUNKS = ((0, 512), (512, 512), (1024, 512), (1536, 512), (2048, 512), (2560, 128))

TM_PROJ = 512
TM_POST = 512
T_GLA = 512
TK = 256
TQ = 2 * TK
NQ = 2 * TQ
VT_ROWS = DIFF_DV + 16
FF_CHUNKS = ((0, 512), (512, 512), (1024, 512), (1536, 512), (2048, 512), (2560, 256))
MASK_VALUE = -1e30

_NT = (((1,), (1,)), ((), ()))
_TN = (((0,), (0,)), ((), ()))


def _params(*sem):
    return pltpu.CompilerParams(dimension_semantics=sem, vmem_limit_bytes=VMEM_LIMIT_BYTES)


def _rms(x, g):
    ms = jnp.mean(x * x, axis=-1, keepdims=True)
    return x * lax.rsqrt(ms + RMS_EPS) * g


def _silu(x):
    return x * jax.nn.sigmoid(x)


def _mod_kernel(c_ref, w_ref, b_ref, o_ref):
    cond = _silu(c_ref[...]).astype(BF16)
    o_ref[0] = jnp.dot(cond, w_ref[0].astype(BF16), preferred_element_type=F32) + b_ref[0]


def _modulation(c, w_mod, b_mod):
    B = c.shape[0]
    n = w_mod.shape[-1]
    tn = n // 4
    return pl.pallas_call(
        _mod_kernel,
        out_shape=jax.ShapeDtypeStruct((DEPTH, B, n), F32),
        grid=(DEPTH, n // tn),
        in_specs=[
            pl.BlockSpec((B, D_MODEL), lambda l, j: (0, 0)),
            pl.BlockSpec((1, D_MODEL, tn), lambda l, j: (l, 0, j)),
            pl.BlockSpec((1, 1, tn), lambda l, j: (l, 0, j)),
        ],
        out_specs=pl.BlockSpec((1, B, tn), lambda l, j: (l, 0, j)),
        compiler_params=_params("parallel", "parallel"),
        name="adaln_modulation",
    )(c, w_mod, b_mod.reshape(DEPTH, 1, n))


def _inproj_kernel(x_ref, sc_ref, sh_ref, g_ref, w_ref, wvt_ref, o_ref, vt_ref):
    h = _rms(x_ref[0], g_ref[...]) * (1.0 + sc_ref[0]) + sh_ref[0]
    hb = h.astype(BF16)
    for off, width in PROJ_CHUNKS:
        o_ref[0, :, off:off + width] = jnp.dot(
            hb, w_ref[:, off:off + width], preferred_element_type=F32).astype(BF16)
    vt = lax.dot_general(wvt_ref[...], hb, _NT, preferred_element_type=F32)
    ones = jnp.ones((VT_ROWS - DIFF_DV, TK), BF16)
    for t in range(TM_PROJ // TK):
        for h in range(DIFF_HEADS):
            vt_ref[0, t, h * VT_ROWS:h * VT_ROWS + DIFF_DV, :] = vt[
                h * DIFF_DV:(h + 1) * DIFF_DV, t * TK:(t + 1) * TK].astype(BF16)
            vt_ref[0, t, h * VT_ROWS + DIFF_DV:(h + 1) * VT_ROWS, :] = ones


def _inproj(x, sc, sh, gain, w, wvt):
    B, S, D = x.shape
    vec = pl.BlockSpec((1, 1, D), lambda b, i: (b, 0, 0))
    nv = DIFF_HEADS * VT_ROWS
    return pl.pallas_call(
        _inproj_kernel,
        out_shape=(jax.ShapeDtypeStruct((B, S, PROJ_W), BF16),
                   jax.ShapeDtypeStruct((B, S // TK, nv, TK), BF16)),
        grid=(B, S // TM_PROJ),
        in_specs=[
            pl.BlockSpec((1, TM_PROJ, D), lambda b, i: (b, i, 0)),
            vec, vec,
            pl.BlockSpec((1, D), lambda b, i: (0, 0)),
            pl.BlockSpec((D, PROJ_W), lambda b, i: (0, 0), pipeline_mode=pl.Buffered(1)),
            pl.BlockSpec(wvt.shape, lambda b, i: (0, 0), pipeline_mode=pl.Buffered(1)),
        ],
        out_specs=(pl.BlockSpec((1, TM_PROJ, PROJ_W), lambda b, i: (b, i, 0)),
                   pl.BlockSpec((1, TM_PROJ // TK, nv, TK), lambda b, i: (b, i, 0, 0))),
        compiler_params=_params("parallel", "parallel"),
        name="in_projection",
    )(x, sc, sh, gain, w, wvt)


def _gla_kernel(q_ref, k_ref, v_ref, r_ref, a_ref, wg_ref, bg_ref, gn_ref, o_ref,
                st_ref, qe_ref, ke_ref, kd_ref):
    C = GLA_CHUNK
    n_chunks = T_GLA // C
    n_pairs = GLA_HEADS // 2

    @pl.when(pl.program_id(1) == 0)
    def _():
        st_ref[...] = jnp.zeros_like(st_ref)

    z = jnp.dot(a_ref[0], wg_ref[...], preferred_element_type=F32) + bg_ref[...]
    la = (jnp.minimum(z, 0.0) - jnp.log(1.0 + jnp.exp(-jnp.abs(z)))) * (1.0 / GLA_TAU)
    la_hi = la.astype(BF16)
    la_lo = (la - la_hi.astype(F32)).astype(BF16)
    row = lax.broadcasted_iota(jnp.int32, (C, C), 0)
    col = lax.broadcasted_iota(jnp.int32, (C, C), 1)
    ltri = jnp.where(row >= col, 1.0, 0.0).astype(BF16)
    q = q_ref[0].astype(F32)
    k = k_ref[0].astype(F32)
    decay = []
    for c in range(n_chunks):
        sl = slice(c * C, (c + 1) * C)
        b = (jnp.dot(ltri, la_hi[sl], preferred_element_type=F32)
             + jnp.dot(ltri, la_lo[sl], preferred_element_type=F32))
        b_last = b[C - 1:C, :]
        qe_ref[sl, :] = (q[sl] * (GLA_DK ** -0.5) * jnp.exp(b)).astype(BF16)
        ke_ref[sl, :] = (k[sl] * jnp.exp(-b)).astype(BF16)
        kd_ref[sl, :] = (k[sl] * jnp.exp(b_last - b)).astype(BF16)
        decay.append(jnp.exp(b_last))

    first_head = lax.broadcasted_iota(jnp.int32, (C, LANES), 1) < GLA_DK
    st_first = lax.broadcasted_iota(jnp.int32, (GLA_DV, LANES), 1) < GLA_DK
    ri = lax.broadcasted_iota(jnp.int32, (2 * C, 2 * C), 0)
    ci = lax.broadcasted_iota(jnp.int32, (2 * C, 2 * C), 1)
    pair_mask = jnp.logical_and((ri >= C) == (ci >= C), ri >= ci)

    states = [st_ref[p] for p in range(n_pairs)]
    for c in range(n_chunks):
        rows = slice(c * C, (c + 1) * C)
        for p in range(n_pairs):
            cols = slice(p * LANES, (p + 1) * LANES)
            pair = slice(2 * p * GLA_DV, (2 * p + 2) * GLA_DV)
            qe = qe_ref[rows, cols]
            ke = ke_ref[rows, cols]
            kd = kd_ref[rows, cols]
            zero = jnp.zeros_like(qe)
            qm = jnp.concatenate([jnp.where(first_head, qe, zero), jnp.where(first_head, zero, qe)], axis=0)
            a = lax.dot_general(qm, jnp.concatenate([ke, ke], axis=0), _NT, preferred_element_type=F32)
            a = jnp.where(pair_mask, a, 0.0).astype(BF16)
            v_pair = v_ref[0, rows, pair]
            v_stack = jnp.concatenate([v_pair[:, :GLA_DV], v_pair[:, GLA_DV:]], axis=0)
            st = states[p]
            o = (jnp.dot(a, v_stack, preferred_element_type=F32)
                 + lax.dot_general(qm, st.astype(BF16), _NT, preferred_element_type=F32))
            ut = lax.dot_general(v_pair, kd, _TN, preferred_element_type=F32)
            states[p] = st * decay[c][:, cols] + jnp.where(st_first, ut[:GLA_DV], ut[GLA_DV:])
            r_pair = r_ref[0, rows, pair]
            gate = _silu(jnp.concatenate([r_pair[:, :GLA_DV], r_pair[:, GLA_DV:]], axis=0).astype(F32))
            on = (_rms(o, gn_ref[...]) * gate).astype(BF16)
            o_ref[0, rows, 2 * p * GLA_DV:(2 * p + 1) * GLA_DV] = on[:C]
            o_ref[0, rows, (2 * p + 1) * GLA_DV:(2 * p + 2) * GLA_DV] = on[C:]
    for p in range(n_pairs):
        st_ref[p] = states[p]


def _gla(proj, wg, bg, gn):
    B, S, _ = proj.shape
    T = T_GLA

    def col_block(width, off):
        return pl.BlockSpec((1, T, width), lambda b, i: (b, i, off // width))

    return pl.pallas_call(
        _gla_kernel,
        out_shape=jax.ShapeDtypeStruct((B, S, GLA_HEADS * GLA_DV), BF16),
        grid=(B, S // T),
        in_specs=[
            col_block(256, OFF_GQ), col_block(256, OFF_GK),
            col_block(512, OFF_GV), col_block(512, OFF_GR),
            col_block(LANES, OFF_GA),
            pl.BlockSpec((LANES, GLA_HEADS * GLA_DK), lambda b, i: (0, 0)),
            pl.BlockSpec((1, GLA_HEADS * GLA_DK), lambda b, i: (0, 0)),
            pl.BlockSpec((1, GLA_DV), lambda b, i: (0, 0)),
        ],
        out_specs=pl.BlockSpec((1, T, GLA_HEADS * GLA_DV), lambda b, i: (b, i, 0)),
        scratch_shapes=[
            pltpu.VMEM((GLA_HEADS // 2, GLA_DV, LANES), F32),
            pltpu.VMEM((T, GLA_HEADS * GLA_DK), BF16),
            pltpu.VMEM((T, GLA_HEADS * GLA_DK), BF16),
            pltpu.VMEM((T, GLA_HEADS * GLA_DK), BF16),
        ],
        compiler_params=_params("parallel", "arbitrary"),
        name="gla_chunked",
    )(proj, proj, proj, proj, proj, wg, bg, gn)


def _t5_bucket_np(rel):
    max_exact = REL_BUCKETS // 2
    n = np.maximum(rel, 0)
    nf = np.maximum(n, 1).astype(np.float32)
    large = max_exact + (np.log(nf / max_exact) / math.log(REL_MAX_DIST / max_exact)
                         * (REL_BUCKETS - max_exact)).astype(np.int32)
    large = np.minimum(large, REL_BUCKETS - 1)
    return np.where(n < max_exact, n, large).astype(np.int32)


def _bias_kernel(rb_ref, idx_ref, o_ref):
    h = pl.program_id(1)
    idx = idx_ref[0]
    far = rb_ref[REL_BUCKETS - 1, h]
    out = jnp.zeros(idx.shape, F32)
    for bkt in range(REL_BUCKETS - 1):
        out = jnp.where(idx == bkt, rb_ref[bkt, h] - far, out)
    o_ref[0, 0] = jnp.where(idx < 0, MASK_VALUE, out * LOG2E)


def _bias_blocks(rel_bias):
    key = np.arange(TK)[:, None]
    qry = np.arange(TK)[None, :]
    idx_diag = np.where(qry - key >= 0, _t5_bucket_np(qry - key), -1)
    idx_next = _t5_bucket_np(qry - key + TK)
    idx = jnp.asarray(np.stack([idx_diag, idx_next]).astype(np.int32))
    return pl.pallas_call(
        _bias_kernel,
        out_shape=jax.ShapeDtypeStruct((2, DIFF_HEADS, TK, TK), F32),
        grid=(2, DIFF_HEADS),
        in_specs=[
            pl.BlockSpec(memory_space=pltpu.SMEM),
            pl.BlockSpec((1, TK, TK), lambda t, h: (t, 0, 0)),
        ],
        out_specs=pl.BlockSpec((1, 1, TK, TK), lambda t, h: (t, h, 0, 0)),
        compiler_params=_params("parallel", "parallel"),
        name="t5_bias_blocks",
    )(rel_bias.astype(F32), idx)


def _attn_kernel(dl_ref, q_ref, k_ref, vt_ref, bias_ref, dn_ref, o_ref,
                 qs_ref, sa_ref, sb_ref, m_ref, acc_ref, *, lam_init):
    qi = pl.program_id(1)
    heads = range(DIFF_HEADS)
    hs = [slice(h * LANES, (h + 1) * LANES) for h in heads]
    feat = lax.broadcasted_iota(jnp.int32, (LANES, TQ), 0)
    for h in heads:
        qt = q_ref[0, :, hs[h]].astype(F32).T
        zero = jnp.zeros_like(qt)
        qs_ref[h, :, 0:TQ] = jnp.where(feat < DIFF_DQK, qt, zero).astype(BF16)
        qs_ref[h, :, TQ:NQ] = jnp.where(feat >= DIFF_DQK, qt, zero).astype(BF16)
    m_ref[...] = jnp.full(m_ref.shape, MASK_VALUE, F32)
    acc_ref[...] = jnp.zeros_like(acc_ref)

    def scores(j, dst_ref):
        rows = pl.ds(pl.multiple_of(j * TK, TK), TK)
        for h in heads:
            dst_ref[h] = jnp.dot(k_ref[0, rows, hs[h]], qs_ref[h], preferred_element_type=F32)

    def add_bias(s, kind, h):
        same, nxt = bias_ref[0, h], bias_ref[1, h]
        blocks = [s[:, c * TK:(c + 1) * TK] for c in range(NQ // TK)]
        masked = jnp.full((TK, TK), MASK_VALUE, F32)
        for c in (0, 2):
            if kind == 0:
                blocks[c] = blocks[c] + nxt
            elif kind == 1:
                blocks[c] = blocks[c] + same
                blocks[c + 1] = blocks[c + 1] + nxt
            else:
                blocks[c] = masked
                blocks[c + 1] = blocks[c + 1] + same
        return jnp.concatenate(blocks, axis=1)

    def consume(j, src_ref, kind):
        p, alpha = [], []
        for h in heads:
            s = src_ref[h] if kind is None else add_bias(src_ref[h], kind, h)
            m_prev = m_ref[h]
            m_next = jnp.maximum(m_prev, jnp.max(s, axis=0, keepdims=True))
            m_ref[h] = m_next
            alpha.append(jnp.exp2(m_prev - m_next))
            p.append(jnp.exp2(s - m_next).astype(BF16))
        for h in heads:
            acc_ref[h] = acc_ref[h] * alpha[h] + jnp.dot(
                vt_ref[0, j, h * VT_ROWS:(h + 1) * VT_ROWS, :], p[h], preferred_element_type=F32)

    scores(0, sa_ref)

    @pl.when(qi == 0)
    def _():
        scores(1, sb_ref)
        consume(0, sa_ref, 1)
        consume(1, sb_ref, 2)

    @pl.when(qi >= 1)
    def _():
        def pair(i, carry):
            j = 2 * i
            scores(j + 1, sb_ref)
            consume(j, sa_ref, None)
            scores(j + 2, sa_ref)
            consume(j + 1, sb_ref, None)
            return carry

        lax.fori_loop(0, qi - 1, pair, 0)
        j = 2 * qi - 2
        scores(j + 1, sb_ref)
        consume(j, sa_ref, None)
        scores(j + 2, sa_ref)
        consume(j + 1, sb_ref, 0)
        scores(j + 3, sb_ref)
        consume(j + 2, sa_ref, 1)
        consume(j + 3, sb_ref, 2)

    dl = dl_ref[...]
    lam = (jnp.exp(jnp.sum(dl[0:1] * dl[1:2], axis=1, keepdims=True))
           - jnp.exp(jnp.sum(dl[2:3] * dl[3:4], axis=1, keepdims=True)) + lam_init)
    for h in heads:
        acc = acc_ref[h, 0:DIFF_DV, :]
        inv = 1.0 / acc_ref[h, DIFF_DV:DIFF_DV + 1, :]
        o = acc[:, 0:TQ] * inv[:, 0:TQ] - lam * (acc[:, TQ:NQ] * inv[:, TQ:NQ])
        ms = jnp.mean(o * o, axis=0, keepdims=True)
        on = o * lax.rsqrt(ms + RMS_EPS) * (dn_ref[...] * (1.0 - lam_init))
        o_ref[0, :, h * DIFF_DV:(h + 1) * DIFF_DV] = on.T.astype(BF16)


def _diff_attention(proj, vt, bias, diff_lambda, diff_norm_col, lam_init):
    B, S, _ = proj.shape
    width = DIFF_HEADS * LANES
    return pl.pallas_call(
        functools.partial(_attn_kernel, lam_init=lam_init),
        out_shape=jax.ShapeDtypeStruct((B, S, DIFF_HEADS * DIFF_DV), BF16),
        grid=(B, S // TQ),
        in_specs=[
            pl.BlockSpec((4, DIFF_DQK), lambda b, i: (0, 0)),
            pl.BlockSpec((1, TQ, width), lambda b, i: (b, i, OFF_DQ // width)),
            pl.BlockSpec((1, S, width), lambda b, i: (b, 0, OFF_DK // width)),
            pl.BlockSpec((1,) + vt.shape[1:], lambda b, i: (b, 0, 0, 0)),
            pl.BlockSpec(bias.shape, lambda b, i: (0, 0, 0, 0), pipeline_mode=pl.Buffered(1)),
            pl.BlockSpec((DIFF_DV, 1), lambda b, i: (0, 0)),
        ],
        out_specs=pl.BlockSpec((1, TQ, DIFF_HEADS * DIFF_DV), lambda b, i: (b, i, 0)),
        scratch_shapes=[
            pltpu.VMEM((DIFF_HEADS, LANES, NQ), BF16),
            pltpu.VMEM((DIFF_HEADS, TK, NQ), F32),
            pltpu.VMEM((DIFF_HEADS, TK, NQ), F32),
            pltpu.VMEM((DIFF_HEADS, 1, NQ), F32),
            pltpu.VMEM((DIFF_HEADS, VT_ROWS, NQ), F32),
        ],
        compiler_params=_params("parallel", "arbitrary"),
        name="diff_attention",
    )(diff_lambda, proj, proj, vt, bias, diff_norm_col)


def _post_kernel(og_ref, od_ref, x_ref, g1_ref, sc2_ref, sh2_ref, g2_ref, n_ref,
                 wo_ref, wgu_ref, wd_ref, xo_ref):
    half = GLA_HEADS * GLA_DV
    y = (jnp.dot(og_ref[0], wo_ref[0:half, :], preferred_element_type=F32)
         + jnp.dot(od_ref[0], wo_ref[half:2 * half, :], preferred_element_type=F32))
    x1 = x_ref[0] + g1_ref[0] * _rms(y, n_ref[0:1, :])
    h = (_rms(x1, n_ref[1:2, :]) * (1.0 + sc2_ref[0]) + sh2_ref[0]).astype(BF16)
    acc = jnp.zeros(x1.shape, F32)
    for off, width in FF_CHUNKS:
        gate = jnp.dot(h, wgu_ref[:, off:off + width], preferred_element_type=F32)
        up = jnp.dot(h, wgu_ref[:, D_FF + off:D_FF + off + width], preferred_element_type=F32)
        act = (_silu(gate) * up).astype(BF16)
        acc = acc + jnp.dot(act, wd_ref[off:off + width, :], preferred_element_type=F32)
    xo_ref[0] = x1 + g2_ref[0] * _rms(acc, n_ref[2:3, :])


def _post(og, od, x, g1, sc2, sh2, g2, gains, wo, wgu, wd):
    B, S, D = x.shape
    tm = TM_POST
    vec = pl.BlockSpec((1, 1, D), lambda b, i: (b, 0, 0))

    def resident(shape):
        return pl.BlockSpec(shape, lambda b, i: (0,) * len(shape), pipeline_mode=pl.Buffered(1))

    return pl.pallas_call(
        _post_kernel,
        out_shape=jax.ShapeDtypeStruct((B, S, D), F32),
        grid=(B, S // tm),
        in_specs=[
            pl.BlockSpec((1, tm, og.shape[-1]), lambda b, i: (b, i, 0)),
            pl.BlockSpec((1, tm, od.shape[-1]), lambda b, i: (b, i, 0)),
            pl.BlockSpec((1, tm, D), lambda b, i: (b, i, 0)),
            vec, vec, vec, vec,
            pl.BlockSpec((3, D), lambda b, i: (0, 0)),
            resident(wo.shape), resident(wgu.shape), resident(wd.shape),
        ],
        out_specs=pl.BlockSpec((1, tm, D), lambda b, i: (b, i, 0)),
        compiler_params=_params("parallel", "parallel"),
        name="outproj_ffn",
    )(og, od, x, g1, sc2, sh2, g2, gains, wo, wgu, wd)


def kernel(x, c, w_mod, b_mod, norm_gains, w_in, w_gla_gate, b_gla_gate, gla_norm, diff_lambda,
           diff_norm, rel_bias, w_out, w_gate_up, w_down):
    B, S, D = x.shape
    assert D == D_MODEL and S % T_GLA == 0 and S % TQ == 0 and S % TM_PROJ == 0 and S % TM_POST == 0
    f32 = lambda a: a.astype(F32)

    n_g = OFF_DQ
    w_in = f32(w_in)
    dq_scale = (DIFF_DQK ** -0.5) * LOG2E
    w_diff = w_in[:, :, n_g + GLA_LOWRANK:]
    w_in_p = jnp.concatenate(
        [w_in[:, :, :n_g], w_diff[:, :, :512] * dq_scale, w_diff[:, :, 512:1024],
         w_in[:, :, n_g:n_g + GLA_LOWRANK],
         jnp.zeros((DEPTH, D, PROJ_W - OFF_GA - GLA_LOWRANK), F32)], axis=-1).astype(BF16)
    assert w_in_p.shape[-1] == PROJ_W
    w_dvt = jnp.swapaxes(w_diff[:, :, 1024:], 1, 2).astype(BF16)
    wg_p = jnp.concatenate(
        [f32(w_gla_gate), jnp.zeros((DEPTH, LANES - GLA_LOWRANK, GLA_HEADS * GLA_DK), F32)],
        axis=1).astype(BF16)
    w_out_b = w_out.astype(BF16)
    w_gu_b = w_gate_up.astype(BF16)
    w_down_b = w_down.astype(BF16)

    mod = _modulation(f32(c), f32(w_mod), f32(b_mod))
    bias = _bias_blocks(rel_bias)

    x = f32(x)
    for l in range(DEPTH):
        lam_init = 0.8 - 0.6 * math.exp(-0.3 * l)
        m = mod[l].reshape(B, 6, 1, D)
        sh1, sc1, g1, sh2, sc2, g2 = (m[:, i] for i in range(6))
        gains = f32(norm_gains[l])
        proj, vt = _inproj(x, sc1, sh1, gains[0:1], w_in_p[l], w_dvt[l])
        og = _gla(proj, wg_p[l], f32(b_gla_gate[l]).reshape(1, -1), f32(gla_norm[l]).reshape(1, -1))
        od = _diff_attention(proj, vt, bias, f32(diff_lambda[l]), f32(diff_norm[l]).reshape(-1, 1), lam_init)
        x = _post(og, od, x, g1, sc2, sh2, g2, gains[1:4], w_out_b[l], w_gu_b[l], w_down_b[l])
    return x
```

```python
import functools
import math

import numpy as np
import jax
import jax.numpy as jnp
from jax import lax
from jax.experimental import pallas as pl
from jax.experimental.pallas import tpu as pltpu

F32 = jnp.float32
BF16 = jnp.bfloat16

D_MODEL = 1024
DEPTH = 4
GLA_HEADS = 4
GLA_DV = 128
GLA_DK = 64
GLA_LOWRANK = 16
GLA_TAU = 16.0
GLA_CHUNK = 64
DIFF_HEADS = 4
DIFF_DV = 128
DIFF_DQK = 64
REL_BUCKETS = 32
REL_MAX_DIST = 128
D_FF = 2816
RMS_EPS = 1e-6
LOG2E = 1.4426950408889634

V7X_VMEM_BYTES = 64 * 1024 * 1024
VMEM_LIMIT_BYTES = V7X_VMEM_BYTES - 8 * 1024 * 1024
LANES = 128

OFF_GQ, OFF_GK, OFF_GV, OFF_GR = 0, 256, 512, 1024
OFF_DQ, OFF_DK, OFF_GA = 1536, 2048, 2560
PROJ_W = 2688
PROJ_CHUNKS = ((0, 512), (512, 512), (1024, 512), (1536, 512), (2048, 512), (2560, 128))

TM_PROJ = 512
TM_POST = 512
T_GLA = 512
TK = 256
TQ = 2 * TK
NQ = 2 * TQ
VT_ROWS = DIFF_DV + 16
FF_CHUNKS = ((0, 512), (512, 512), (1024, 512), (1536, 512), (2048, 512), (2560, 256))
MASK_VALUE = -1e30

_NT = (((1,), (1,)), ((), ()))
_TN = (((0,), (0,)), ((), ()))


def _params(*sem):
    return pltpu.CompilerParams(dimension_semantics=sem, vmem_limit_bytes=VMEM_LIMIT_BYTES)


def _rms(x, g):
    ms = jnp.mean(x * x, axis=-1, keepdims=True)
    return x * lax.rsqrt(ms + RMS_EPS) * g


def _silu(x):
    return x * jax.nn.sigmoid(x)


def _mod_kernel(c_ref, w_ref, b_ref, o_ref):
    cond = _silu(c_ref[...]).astype(BF16)
    o_ref[0] = jnp.dot(cond, w_ref[0].astype(BF16), preferred_element_type=F32) + b_ref[0]


def _modulation(c, w_mod, b_mod):
    B = c.shape[0]
    n = w_mod.shape[-1]
    tn = n // 4
    return pl.pallas_call(
        _mod_kernel,
        out_shape=jax.ShapeDtypeStruct((DEPTH, B, n), F32),
        grid=(DEPTH, n // tn),
        in_specs=[
            pl.BlockSpec((B, D_MODEL), lambda l, j: (0, 0)),
            pl.BlockSpec((1, D_MODEL, tn), lambda l, j: (l, 0, j)),
            pl.BlockSpec((1, 1, tn), lambda l, j: (l, 0, j)),
        ],
        out_specs=pl.BlockSpec((1, B, tn), lambda l, j: (l, 0, j)),
        compiler_params=_params("parallel", "parallel"),
        name="adaln_modulation",
    )(c, w_mod, b_mod.reshape(DEPTH, 1, n))


def _inproj_kernel(x_ref, sc_ref, sh_ref, g_ref, w_ref, wvt_ref, o_ref, vt_ref):
    h = _rms(x_ref[0], g_ref[...]) * (1.0 + sc_ref[0]) + sh_ref[0]
    hb = h.astype(BF16)
    for off, width in PROJ_CHUNKS:
        o_ref[0, :, off:off + width] = jnp.dot(
            hb, w_ref[:, off:off + width], preferred_element_type=F32).astype(BF16)
    vt = lax.dot_general(wvt_ref[...], hb, _NT, preferred_element_type=F32)
    ones = jnp.ones((VT_ROWS - DIFF_DV, TK), BF16)
    for t in range(TM_PROJ // TK):
        for h in range(DIFF_HEADS):
            vt_ref[0, t, h * VT_ROWS:h * VT_ROWS + DIFF_DV, :] = vt[
                h * DIFF_DV:(h + 1) * DIFF_DV, t * TK:(t + 1) * TK].astype(BF16)
            vt_ref[0, t, h * VT_ROWS + DIFF_DV:(h + 1) * VT_ROWS, :] = ones


def _inproj(x, sc, sh, gain, w, wvt):
    B, S, D = x.shape
    vec = pl.BlockSpec((1, 1, D), lambda b, i: (b, 0, 0))
    nv = DIFF_HEADS * VT_ROWS
    return pl.pallas_call(
        _inproj_kernel,
        out_shape=(jax.ShapeDtypeStruct((B, S, PROJ_W), BF16),
                   jax.ShapeDtypeStruct((B, S // TK, nv, TK), BF16)),
        grid=(B, S // TM_PROJ),
        in_specs=[
            pl.BlockSpec((1, TM_PROJ, D), lambda b, i: (b, i, 0)),
            vec, vec,
            pl.BlockSpec((1, D), lambda b, i: (0, 0)),
            pl.BlockSpec((D, PROJ_W), lambda b, i: (0, 0), pipeline_mode=pl.Buffered(1)),
            pl.BlockSpec(wvt.shape, lambda b, i: (0, 0), pipeline_mode=pl.Buffered(1)),
        ],
        out_specs=(pl.BlockSpec((1, TM_PROJ, PROJ_W), lambda b, i: (b, i, 0)),
                   pl.BlockSpec((1, TM_PROJ // TK, nv, TK), lambda b, i: (b, i, 0, 0))),
        compiler_params=_params("parallel", "parallel"),
        name="in_projection",
    )(x, sc, sh, gain, w, wvt)


def _gla_kernel(q_ref, k_ref, v_ref, r_ref, a_ref, wg_ref, bg_ref, gn_ref, o_ref,
                st_ref, qe_ref, ke_ref, kd_ref):
    C = GLA_CHUNK
    n_chunks = T_GLA // C
    n_pairs = GLA_HEADS // 2

    @pl.when(pl.program_id(1) == 0)
    def _():
        st_ref[...] = jnp.zeros_like(st_ref)

    z = jnp.dot(a_ref[0], wg_ref[...], preferred_element_type=F32) + bg_ref[...]
    la = (jnp.minimum(z, 0.0) - jnp.log(1.0 + jnp.exp(-jnp.abs(z)))) * (1.0 / GLA_TAU)
    la_hi = la.astype(BF16)
    la_lo = (la - la_hi.astype(F32)).astype(BF16)
    row = lax.broadcasted_iota(jnp.int32, (C, C), 0)
    col = lax.broadcasted_iota(jnp.int32, (C, C), 1)
    ltri = jnp.where(row >= col, 1.0, 0.0).astype(BF16)
    q = q_ref[0].astype(F32)
    k = k_ref[0].astype(F32)
    decay = []
    for c in range(n_chunks):
        sl = slice(c * C, (c + 1) * C)
        b = (jnp.dot(ltri, la_hi[sl], preferred_element_type=F32)
             + jnp.dot(ltri, la_lo[sl], preferred_element_type=F32))
        b_last = b[C - 1:C, :]
        qe_ref[sl, :] = (q[sl] * (GLA_DK ** -0.5) * jnp.exp(b)).astype(BF16)
        ke_ref[sl, :] = (k[sl] * jnp.exp(-b)).astype(BF16)
        kd_ref[sl, :] = (k[sl] * jnp.exp(b_last - b)).astype(BF16)
        decay.append(jnp.exp(b_last))

    first_head = lax.broadcasted_iota(jnp.int32, (C, LANES), 1) < GLA_DK
    st_first = lax.broadcasted_iota(jnp.int32, (GLA_DV, LANES), 1) < GLA_DK
    ri = lax.broadcasted_iota(jnp.int32, (2 * C, 2 * C), 0)
    ci = lax.broadcasted_iota(jnp.int32, (2 * C, 2 * C), 1)
    pair_mask = jnp.logical_and((ri >= C) == (ci >= C), ri >= ci)

    states = [st_ref[p] for p in range(n_pairs)]
    for c in range(n_chunks):
        rows = slice(c * C, (c + 1) * C)
        for p in range(n_pairs):
            cols = slice(p * LANES, (p + 1) * LANES)
            pair = slice(2 * p * GLA_DV, (2 * p + 2) * GLA_DV)
            qe = qe_ref[rows, cols]
            ke = ke_ref[rows, cols]
            kd = kd_ref[rows, cols]
            zero = jnp.zeros_like(qe)
            qm = jnp.concatenate([jnp.where(first_head, qe, zero), jnp.where(first_head, zero, qe)], axis=0)
            a = lax.dot_general(qm, jnp.concatenate([ke, ke], axis=0), _NT, preferred_element_type=F32)
            a = jnp.where(pair_mask, a, 0.0).astype(BF16)
            v_pair = v_ref[0, rows, pair]
            v_stack = jnp.concatenate([v_pair[:, :GLA_DV], v_pair[:, GLA_DV:]], axis=0)
            st = states[p]
            o = (jnp.dot(a, v_stack, preferred_element_type=F32)
                 + lax.dot_general(qm, st.astype(BF16), _NT, preferred_element_type=F32))
            ut = lax.dot_general(v_pair, kd, _TN, preferred_element_type=F32)
            states[p] = st * decay[c][:, cols] + jnp.where(st_first, ut[:GLA_DV], ut[GLA_DV:])
            r_pair = r_ref[0, rows, pair]
            gate = _silu(jnp.concatenate([r_pair[:, :GLA_DV], r_pair[:, GLA_DV:]], axis=0).astype(F32))
            on = (_rms(o, gn_ref[...]) * gate).astype(BF16)
            o_ref[0, rows, 2 * p * GLA_DV:(2 * p + 1) * GLA_DV] = on[:C]
            o_ref[0, rows, (2 * p + 1) * GLA_DV:(2 * p + 2) * GLA_DV] = on[C:]
    for p in range(n_pairs):
        st_ref[p] = states[p]


def _gla(proj, wg, bg, gn):
    B, S, _ = proj.shape
    T = T_GLA

    def col_block(width, off):
        return pl.BlockSpec((1, T, width), lambda b, i: (b, i, off // width))

    return pl.pallas_call(
        _gla_kernel,
        out_shape=jax.ShapeDtypeStruct((B, S, GLA_HEADS * GLA_DV), BF16),
        grid=(B, S // T),
        in_specs=[
            col_block(256, OFF_GQ), col_block(256, OFF_GK),
            col_block(512, OFF_GV), col_block(512, OFF_GR),
            col_block(LANES, OFF_GA),
            pl.BlockSpec((LANES, GLA_HEADS * GLA_DK), lambda b, i: (0, 0)),
            pl.BlockSpec((1, GLA_HEADS * GLA_DK), lambda b, i: (0, 0)),
            pl.BlockSpec((1, GLA_DV), lambda b, i: (0, 0)),
        ],
        out_specs=pl.BlockSpec((1, T, GLA_HEADS * GLA_DV), lambda b, i: (b, i, 0)),
        scratch_shapes=[
            pltpu.VMEM((GLA_HEADS // 2, GLA_DV, LANES), F32),
            pltpu.VMEM((T, GLA_HEADS * GLA_DK), BF16),
            pltpu.VMEM((T, GLA_HEADS * GLA_DK), BF16),
            pltpu.VMEM((T, GLA_HEADS * GLA_DK), BF16),
        ],
        compiler_params=_params("parallel", "arbitrary"),
        name="gla_chunked",
    )(proj, proj, proj, proj, proj, wg, bg, gn)


def _t5_bucket_np(rel):
    max_exact = REL_BUCKETS // 2
    n = np.maximum(rel, 0)
    nf = np.maximum(n, 1).astype(np.float32)
    large = max_exact + (np.log(nf / max_exact) / math.log(REL_MAX_DIST / max_exact)
                         * (REL_BUCKETS - max_exact)).astype(np.int32)
    large = np.minimum(large, REL_BUCKETS - 1)
    return np.where(n < max_exact, n, large).astype(np.int32)


def _bias_kernel(rb_ref, idx_ref, o_ref):
    h = pl.program_id(1)
    idx = idx_ref[0]
    far = rb_ref[REL_BUCKETS - 1, h]
    out = jnp.zeros(idx.shape, F32)
    for bkt in range(REL_BUCKETS - 1):
        out = jnp.where(idx == bkt, rb_ref[bkt, h] - far, out)
    o_ref[0, 0] = jnp.where(idx < 0, MASK_VALUE, out * LOG2E)


def _bias_blocks(rel_bias):
    key = np.arange(TK)[:, None]
    qry = np.arange(TK)[None, :]
    idx_diag = np.where(qry - key >= 0, _t5_bucket_np(qry - key), -1)
    idx_next = _t5_bucket_np(qry - key + TK)
    idx = jnp.asarray(np.stack([idx_diag, idx_next]).astype(np.int32))
    return pl.pallas_call(
        _bias_kernel,
        out_shape=jax.ShapeDtypeStruct((2, DIFF_HEADS, TK, TK), F32),
        grid=(2, DIFF_HEADS),
        in_specs=[
            pl.BlockSpec(memory_space=pltpu.SMEM),
            pl.BlockSpec((1, TK, TK), lambda t, h: (t, 0, 0)),
        ],
        out_specs=pl.BlockSpec((1, 1, TK, TK), lambda t, h: (t, h, 0, 0)),
        compiler_params=_params("parallel", "parallel"),
        name="t5_bias_blocks",
    )(rel_bias.astype(F32), idx)


def _attn_kernel(dl_ref, q_ref, k_ref, vt_ref, bias_ref, dn_ref, o_ref,
                 qs_ref, sa_ref, sb_ref, m_ref, acc_ref, *, lam_init):
    qi = pl.program_id(1)
    heads = range(DIFF_HEADS)
    hs = [slice(h * LANES, (h + 1) * LANES) for h in heads]
    feat = lax.broadcasted_iota(jnp.int32, (LANES, TQ), 0)
    for h in heads:
        qt = q_ref[0, :, hs[h]].astype(F32).T
        zero = jnp.zeros_like(qt)
        qs_ref[h, :, 0:TQ] = jnp.where(feat < DIFF_DQK, qt, zero).astype(BF16)
        qs_ref[h, :, TQ:NQ] = jnp.where(feat >= DIFF_DQK, qt, zero).astype(BF16)
    m_ref[...] = jnp.full(m_ref.shape, MASK_VALUE, F32)
    acc_ref[...] = jnp.zeros_like(acc_ref)

    def scores(j, dst_ref):
        rows = pl.ds(pl.multiple_of(j * TK, TK), TK)
        for h in heads:
            dst_ref[h] = jnp.dot(k_ref[0, rows, hs[h]], qs_ref[h], preferred_element_type=F32)

    def add_bias(s, kind, h):
        same, nxt = bias_ref[0, h], bias_ref[1, h]
        blocks = [s[:, c * TK:(c + 1) * TK] for c in range(NQ // TK)]
        for c in (0, 2):
            if kind == 0:
                blocks[c] = blocks[c] + nxt
            else:
                blocks[c] = blocks[c] + same
                blocks[c + 1] = blocks[c + 1] + nxt
        return jnp.concatenate(blocks, axis=1)

    late = (slice(TK, TQ), slice(TQ + TK, NQ))

    def scores_last(j, dst_ref):
        rows = pl.ds(pl.multiple_of(j * TK, TK), TK)
        for h in heads:
            w = jnp.concatenate([qs_ref[h, :, c] for c in late], axis=1)
            dst_ref[h, :, 0:TQ] = jnp.dot(k_ref[0, rows, hs[h]], w, preferred_element_type=F32)

    def consume_last(j, src_ref):
        p, alpha = [], []
        for h in heads:
            same = bias_ref[0, h]
            s = src_ref[h, :, 0:TQ] + jnp.concatenate([same, same], axis=1)
            m_prev = jnp.concatenate([m_ref[h, :, c] for c in late], axis=1)
            m_next = jnp.maximum(m_prev, jnp.max(s, axis=0, keepdims=True))
            alpha.append(jnp.exp2(m_prev - m_next))
            p.append(jnp.exp2(s - m_next).astype(BF16))
        for h in heads:
            pv = jnp.dot(vt_ref[0, j, h * VT_ROWS:(h + 1) * VT_ROWS, :], p[h],
                         preferred_element_type=F32)
            for i, c in enumerate(late):
                part = slice(i * TK, (i + 1) * TK)
                acc_ref[h, :, c] = acc_ref[h, :, c] * alpha[h][:, part] + pv[:, part]

    def consume(j, src_ref, kind):
        p, alpha = [], []
        for h in heads:
            s = src_ref[h] if kind is None else add_bias(src_ref[h], kind, h)
            m_prev = m_ref[h]
            m_next = jnp.maximum(m_prev, jnp.max(s, axis=0, keepdims=True))
            m_ref[h] = m_next
            alpha.append(jnp.exp2(m_prev - m_next))
            p.append(jnp.exp2(s - m_next).astype(BF16))
        for h in heads:
            acc_ref[h] = acc_ref[h] * alpha[h] + jnp.dot(
                vt_ref[0, j, h * VT_ROWS:(h + 1) * VT_ROWS, :], p[h], preferred_element_type=F32)

    scores(0, sa_ref)

    @pl.when(qi == 0)
    def _():
        scores_last(1, sb_ref)
        consume(0, sa_ref, 1)
        consume_last(1, sb_ref)

    @pl.when(qi >= 1)
    def _():
        def pair(i, carry):
            j = 2 * i
            scores(j + 1, sb_ref)
            consume(j, sa_ref, None)
            scores(j + 2, sa_ref)
            consume(j + 1, sb_ref, None)
            return carry

        lax.fori_loop(0, qi - 1, pair, 0)
        j = 2 * qi - 2
        scores(j + 1, sb_ref)
        consume(j, sa_ref, None)
        scores(j + 2, sa_ref)
        consume(j + 1, sb_ref, 0)
        scores_last(j + 3, sb_ref)
        consume(j + 2, sa_ref, 1)
        consume_last(j + 3, sb_ref)

    dl = dl_ref[...]
    lam = (jnp.exp(jnp.sum(dl[0:1] * dl[1:2], axis=1, keepdims=True))
           - jnp.exp(jnp.sum(dl[2:3] * dl[3:4], axis=1, keepdims=True)) + lam_init)
    for h in heads:
        acc = acc_ref[h, 0:DIFF_DV, :]
        inv = 1.0 / acc_ref[h, DIFF_DV:DIFF_DV + 1, :]
        o = acc[:, 0:TQ] * inv[:, 0:TQ] - lam * (acc[:, TQ:NQ] * inv[:, TQ:NQ])
        ms = jnp.mean(o * o, axis=0, keepdims=True)
        on = o * lax.rsqrt(ms + RMS_EPS) * (dn_ref[...] * (1.0 - lam_init))
        o_ref[0, :, h * DIFF_DV:(h + 1) * DIFF_DV] = on.T.astype(BF16)


def _diff_attention(proj, vt, bias, diff_lambda, diff_norm_col, lam_init):
    B, S, _ = proj.shape
    width = DIFF_HEADS * LANES
    return pl.pallas_call(
        functools.partial(_attn_kernel, lam_init=lam_init),
        out_shape=jax.ShapeDtypeStruct((B, S, DIFF_HEADS * DIFF_DV), BF16),
        grid=(B, S // TQ),
        in_specs=[
            pl.BlockSpec((4, DIFF_DQK), lambda b, i: (0, 0)),
            pl.BlockSpec((1, TQ, width), lambda b, i: (b, i, OFF_DQ // width)),
            pl.BlockSpec((1, S, width), lambda b, i: (b, 0, OFF_DK // width)),
            pl.BlockSpec((1,) + vt.shape[1:], lambda b, i: (b, 0, 0, 0)),
            pl.BlockSpec(bias.shape, lambda b, i: (0, 0, 0, 0), pipeline_mode=pl.Buffered(1)),
            pl.BlockSpec((DIFF_DV, 1), lambda b, i: (0, 0)),
        ],
        out_specs=pl.BlockSpec((1, TQ, DIFF_HEADS * DIFF_DV), lambda b, i: (b, i, 0)),
        scratch_shapes=[
            pltpu.VMEM((DIFF_HEADS, LANES, NQ), BF16),
            pltpu.VMEM((DIFF_HEADS, TK, NQ), F32),
            pltpu.VMEM((DIFF_HEADS, TK, NQ), F32),
            pltpu.VMEM((DIFF_HEADS, 1, NQ), F32),
            pltpu.VMEM((DIFF_HEADS, VT_ROWS, NQ), F32),
        ],
        compiler_params=_params("parallel", "arbitrary"),
        name="diff_attention",
    )(diff_lambda, proj, proj, vt, bias, diff_norm_col)


def _post_kernel(og_ref, od_ref, x_ref, g1_ref, sc2_ref, sh2_ref, g2_ref, n_ref,
                 wo_ref, wgu_ref, wd_ref, xo_ref, y_ref):
    t = pl.program_id(0)
    last = pl.num_programs(0) - 1
    half = GLA_HEADS * GLA_DV

    def out_projection():
        return (jnp.dot(og_ref[0], wo_ref[0:half, :], preferred_element_type=F32)
                + jnp.dot(od_ref[0], wo_ref[half:2 * half, :], preferred_element_type=F32))

    def residual_ffn(y):
        x1 = x_ref[0] + g1_ref[0] * _rms(y, n_ref[0:1, :])
        h = (_rms(x1, n_ref[1:2, :]) * (1.0 + sc2_ref[0]) + sh2_ref[0]).astype(BF16)
        acc = jnp.zeros(x1.shape, F32)
        for off, width in FF_CHUNKS:
            gate = jnp.dot(h, wgu_ref[:, off:off + width], preferred_element_type=F32)
            up = jnp.dot(h, wgu_ref[:, D_FF + off:D_FF + off + width], preferred_element_type=F32)
            act = (_silu(gate) * up).astype(BF16)
            acc = acc + jnp.dot(act, wd_ref[off:off + width, :], preferred_element_type=F32)
        xo_ref[0] = x1 + g2_ref[0] * _rms(acc, n_ref[2:3, :])

    @pl.when(t == 0)
    def _():
        y_ref[...] = out_projection()

    @pl.when(jnp.logical_and(t > 0, t < last))
    def _():
        y = y_ref[...]
        y_next = out_projection()
        residual_ffn(y)
        y_ref[...] = y_next

    @pl.when(t == last)
    def _():
        residual_ffn(y_ref[...])


def _post(og, od, x, g1, sc2, sh2, g2, gains, wo, wgu, wd):
    B, S, D = x.shape
    tm = TM_POST
    nt = S // tm
    n_tiles = B * nt

    def ahead(t):
        tt = jnp.minimum(t, n_tiles - 1)
        return tt // nt, tt % nt

    def behind(t):
        tp = jnp.maximum(t - 1, 0)
        return tp // nt, tp % nt

    vec = pl.BlockSpec((1, 1, D), lambda t: (behind(t)[0], 0, 0))

    def resident(shape):
        return pl.BlockSpec(shape, lambda t: (0,) * len(shape), pipeline_mode=pl.Buffered(1))

    return pl.pallas_call(
        _post_kernel,
        out_shape=jax.ShapeDtypeStruct((B, S, D), F32),
        grid=(n_tiles + 1,),
        in_specs=[
            pl.BlockSpec((1, tm, og.shape[-1]), lambda t: ahead(t) + (0,)),
            pl.BlockSpec((1, tm, od.shape[-1]), lambda t: ahead(t) + (0,)),
            pl.BlockSpec((1, tm, D), lambda t: behind(t) + (0,)),
            vec, vec, vec, vec,
            pl.BlockSpec((3, D), lambda t: (0, 0)),
            resident(wo.shape), resident(wgu.shape), resident(wd.shape),
        ],
        out_specs=pl.BlockSpec((1, tm, D), lambda t: behind(t) + (0,)),
        scratch_shapes=[pltpu.VMEM((tm, D), F32)],
        compiler_params=_params("arbitrary"),
        name="outproj_ffn",
    )(og, od, x, g1, sc2, sh2, g2, gains, wo, wgu, wd)


def kernel(x, c, w_mod, b_mod, norm_gains, w_in, w_gla_gate, b_gla_gate, gla_norm, diff_lambda,
           diff_norm, rel_bias, w_out, w_gate_up, w_down):
    B, S, D = x.shape
    assert D == D_MODEL and S % T_GLA == 0 and S % TQ == 0 and S % TM_PROJ == 0 and S % TM_POST == 0
    f32 = lambda a: a.astype(F32)

    n_g = OFF_DQ
    w_in = f32(w_in)
    dq_scale = (DIFF_DQK ** -0.5) * LOG2E
    w_diff = w_in[:, :, n_g + GLA_LOWRANK:]
    w_in_p = jnp.concatenate(
        [w_in[:, :, :n_g], w_diff[:, :, :512] * dq_scale, w_diff[:, :, 512:1024],
         w_in[:, :, n_g:n_g + GLA_LOWRANK],
         jnp.zeros((DEPTH, D, PROJ_W - OFF_GA - GLA_LOWRANK), F32)], axis=-1).astype(BF16)
    assert w_in_p.shape[-1] == PROJ_W
    w_dvt = jnp.swapaxes(w_diff[:, :, 1024:], 1, 2).astype(BF16)
    wg_p = jnp.concatenate(
        [f32(w_gla_gate), jnp.zeros((DEPTH, LANES - GLA_LOWRANK, GLA_HEADS * GLA_DK), F32)],
        axis=1).astype(BF16)
    w_out_b = w_out.astype(BF16)
    w_gu_b = w_gate_up.astype(BF16)
    w_down_b = w_down.astype(BF16)

    mod = _modulation(f32(c), f32(w_mod), f32(b_mod))
    bias = _bias_blocks(rel_bias)

    x = f32(x)
    for l in range(DEPTH):
        lam_init = 0.8 - 0.6 * math.exp(-0.3 * l)
        m = mod[l].reshape(B, 6, 1, D)
        sh1, sc1, g1, sh2, sc2, g2 = (m[:, i] for i in range(6))
        gains = f32(norm_gains[l])
        proj, vt = _inproj(x, sc1, sh1, gains[0:1], w_in_p[l], w_dvt[l])
        og = _gla(proj, wg_p[l], f32(b_gla_gate[l]).reshape(1, -1), f32(gla_norm[l]).reshape(1, -1))
        od = _diff_attention(proj, vt, bias, f32(diff_lambda[l]), f32(diff_norm[l]).reshape(-1, 1), lam_init)
        x = _post(og, od, x, g1, sc2, sh2, g2, gains[1:4], w_out_b[l], w_gu_b[l], w_down_b[l])
    return x
```
